```python
import math
import jax, jax.numpy as jnp
from jax import lax
import numpy as np

D_MODEL = 1024
BATCH = 2
SEQ = 8192
DEPTH = 2

GRID_W = 64
CTX_LEN = 256
HEAD_DIM = 64
A_HEADS = 8
A_KV_HEADS = 2
A_GROUP = A_HEADS // A_KV_HEADS
B_HEADS = 4
B_V_DIM = 2 * HEAD_DIM
C_GROUPS = 4
C_GROUP_DIM = 128
N_BRANCH = 3
BRANCH_W = 512
A_Q_W = A_HEADS * HEAD_DIM
A_KV_W = A_KV_HEADS * HEAD_DIM
B_QK_W = B_HEADS * 2 * HEAD_DIM
B_V_W = B_HEADS * B_V_DIM
C_W = C_GROUPS * C_GROUP_DIM
GATE_W = N_BRANCH * D_MODEL
IN_W = A_Q_W + 2 * A_KV_W + 2 * B_QK_W + B_V_W + C_W + GATE_W
D_FF = ((8 * D_MODEL // 3 + 127) // 128) * 128
N_MOD = 9
Q_BLOCK = 128
ROPE_THETA = 10000.0
EPS = 1e-6

kernel_name = "hybrid_gqa_diffattn_fourier_macaron_dit"


def rmsnorm(x, g):
    xf = x.astype(jnp.float32)
    y = xf * lax.rsqrt(jnp.mean(xf * xf, axis=-1, keepdims=True) + EPS)
    return (y * g.astype(jnp.float32)).astype(x.dtype)


def modulate(x, shift, scale):
    return x * (1 + scale[:, None, :]) + shift[:, None, :]


def swiglu(x, wi, wo):
    g, u = jnp.split(x @ wi, 2, axis=-1)
    return (jax.nn.silu(g) * u) @ wo


def ffn_half(h, m, k, g_norm, wi, wo):
    hn = modulate(rmsnorm(h, g_norm), m[3 * k], m[3 * k + 1])
    return h + 0.5 * m[3 * k + 2][:, None, :] * swiglu(hn, wi, wo)


def axial_rope(n_tok):
    rows = n_tok // GRID_W
    row = jnp.repeat(jnp.arange(rows, dtype=jnp.float32), GRID_W)
    col = jnp.tile(jnp.arange(GRID_W, dtype=jnp.float32), rows)
    n_freq = HEAD_DIM // 4
    inv = jnp.power(ROPE_THETA, -jnp.arange(n_freq, dtype=jnp.float32) / n_freq)
    ang = jnp.concatenate([row[:, None] * inv, col[:, None] * inv], axis=-1)
    return jnp.cos(ang), jnp.sin(ang)


def apply_rope(x, cos, sin):
    shp = (1, x.shape[1]) + (1,) * (x.ndim - 3) + (x.shape[-1] // 2,)
    cs = cos.reshape(shp).astype(x.dtype)
    sn = sin.reshape(shp).astype(x.dtype)
    x1, x2 = jnp.split(x, 2, axis=-1)
    return jnp.concatenate([x1 * cs - x2 * sn, x1 * sn + x2 * cs], axis=-1)


def split_projection(z):
    sizes = (A_Q_W, A_KV_W, A_KV_W, B_QK_W, B_QK_W, B_V_W, C_W)
    idx = []
    o = 0
    for s in sizes:
        o += s
        idx.append(o)
    return jnp.split(z, idx, axis=-1)


def project_tokens(hn, w_in, qk_g, cos, sin):
    bn, n = hn.shape[:2]
    qa, ka, va, qb, kb, vb, uc, gl = split_projection(hn @ w_in)
    qa = rmsnorm(qa.reshape(bn, n, A_KV_HEADS, A_GROUP, HEAD_DIM), qk_g[0])
    ka = rmsnorm(ka.reshape(bn, n, A_KV_HEADS, HEAD_DIM), qk_g[1])
    va = va.reshape(bn, n, A_KV_HEADS, HEAD_DIM)
    qb = rmsnorm(qb.reshape(bn, n, B_HEADS, 2, HEAD_DIM), qk_g[2])
    kb = rmsnorm(kb.reshape(bn, n, B_HEADS, 2, HEAD_DIM), qk_g[3])
    vb = vb.reshape(bn, n, B_HEADS, B_V_DIM)
    if cos is not None:
        qa = apply_rope(qa, cos, sin)
        ka = apply_rope(ka, cos, sin)
        qb = apply_rope(qb, cos, sin)
        kb = apply_rope(kb, cos, sin)
    uc = uc.reshape(bn, n, C_GROUPS, C_GROUP_DIM)
    return qa, ka, va, qb, kb, vb, uc, gl


def gqa_attend(q, k, v):
    s = jnp.einsum('bqhgd,bkhd->bhgqk', q.astype(jnp.float32) * HEAD_DIM ** -0.5, k.astype(jnp.float32))
    p = jax.nn.softmax(s, axis=-1)
    return jnp.einsum('bhgqk,bkhd->bqhgd', p.astype(v.dtype), v)


def diff_attend(q, k, v, lam):
    s = jnp.einsum('bqhmd,bkhmd->bhmqk', q.astype(jnp.float32) * HEAD_DIM ** -0.5, k.astype(jnp.float32))
    p = jax.nn.softmax(s, axis=-1)
    p = p[:, :, 0] - lam * p[:, :, 1]
    return jnp.einsum('bhqk,bkhe->bqhe', p.astype(v.dtype), v)


def sweep_query_blocks(fn, q):
    bn, s = q.shape[:2]
    nb = s // Q_BLOCK
    blocks = jnp.moveaxis(q.reshape((bn, nb, Q_BLOCK) + q.shape[2:]), 1, 0)
    out = jnp.moveaxis(lax.map(fn, blocks), 0, 1)
    return out.reshape((bn, s) + out.shape[3:])


def diff_lambda(lp, lam_init):
    lpf = lp.astype(jnp.float32)
    return jnp.exp(jnp.sum(lpf[0] * lpf[1])) - jnp.exp(jnp.sum(lpf[2] * lpf[3])) + lam_init


def fourier_mix(u):
    f = jnp.fft.fft2(u.astype(jnp.float32), axes=(1, 3), norm='ortho')
    return jnp.real(f).astype(u.dtype)


def merge_branches(ya, yb, yc, gl, w_branch, w_out):
    bn, n = ya.shape[:2]
    br = jnp.stack([ya.reshape(bn, n, BRANCH_W), yb.reshape(bn, n, BRANCH_W), yc.reshape(bn, n, BRANCH_W)], axis=2)
    proj = jnp.einsum('bnie,ied->bnid', br, w_branch)
    gates = jax.nn.sigmoid(gl.reshape(bn, n, N_BRANCH, D_MODEL))
    return jnp.sum(gates * proj, axis=2) @ w_out


def setup_inputs(seed: int = 0) -> dict:
    key = jax.random.key(seed)
    ks = jax.random.split(key, 16)
    f32 = jnp.float32
    d = D_MODEL
    x = jax.random.normal(ks[0], (BATCH, SEQ, d), f32)
    c = jax.random.normal(ks[1], (BATCH, d), f32)
    ctx = jax.random.normal(ks[2], (BATCH, CTX_LEN, d), f32)
    c_ctx = jax.random.normal(ks[3], (d,), f32)
    w_ada = jax.random.normal(ks[4], (DEPTH, d, N_MOD * d), f32) * (0.5 * d ** -0.5)
    b_ada = jax.random.normal(ks[5], (DEPTH, N_MOD * d), f32) * 0.02
    norm_g = 1.0 + 0.05 * jax.random.normal(ks[6], (DEPTH, 3, d), f32)
    ffn_wi = jax.random.normal(ks[7], (DEPTH, 2, d, 2 * D_FF), f32) * d ** -0.5
    ffn_wo = jax.random.normal(ks[8], (DEPTH, 2, D_FF, d), f32) * D_FF ** -0.5
    w_in = jax.random.normal(ks[9], (DEPTH, d, IN_W), f32) * d ** -0.5
    qk_g = 1.0 + 0.05 * jax.random.normal(ks[10], (DEPTH, 4, HEAD_DIM), f32)
    diff_lam = 0.1 * jax.random.normal(ks[11], (DEPTH, 4, HEAD_DIM), f32)
    diff_subln_g = 1.0 + 0.05 * jax.random.normal(ks[12], (DEPTH, B_V_DIM), f32)
    w_branch = jax.random.normal(ks[13], (DEPTH, N_BRANCH, BRANCH_W, d), f32) * BRANCH_W ** -0.5
    w_out = jax.random.normal(ks[14], (DEPTH, d, d), f32) * d ** -0.5
    return {"x": x, "c": c, "ctx": ctx, "c_ctx": c_ctx, "w_ada": w_ada, "b_ada": b_ada,
            "norm_g": norm_g, "ffn_wi": ffn_wi, "ffn_wo": ffn_wo, "w_in": w_in, "qk_g": qk_g,
            "diff_lam": diff_lam, "diff_subln_g": diff_subln_g, "w_branch": w_branch, "w_out": w_out}


def reference(x, c, ctx, c_ctx, w_ada, b_ada, norm_g, ffn_wi, ffn_wo, w_in, qk_g,
              diff_lam, diff_subln_g, w_branch, w_out):
    cos, sin = axial_rope(x.shape[1])
    h, hc = x, ctx
    for l in range(DEPTH):
        last = l == DEPTH - 1
        ml = jnp.split(jax.nn.silu(c) @ w_ada[l] + b_ada[l], N_MOD, axis=-1)
        mc = jnp.split(jax.nn.silu(c_ctx)[None] @ w_ada[l] + b_ada[l], N_MOD, axis=-1)

        h = ffn_half(h, ml, 0, norm_g[l, 0], ffn_wi[l, 0], ffn_wo[l, 0])
        hc = ffn_half(hc, mc, 0, norm_g[l, 0], ffn_wi[l, 0], ffn_wo[l, 0])

        hn = modulate(rmsnorm(h, norm_g[l, 1]), ml[3], ml[4])
        hcn = modulate(rmsnorm(hc, norm_g[l, 1]), mc[3], mc[4])
        qa, ka, va, qb, kb, vb, uc, gl = project_tokens(hn, w_in[l], qk_g[l], cos, sin)
        qa_c, ka_c, va_c, qb_c, kb_c, vb_c, uc_c, gl_c = project_tokens(hcn, w_in[l], qk_g[l], None, None)
        lam_init = 0.8 - 0.6 * math.exp(-0.3 * l)
        lam = diff_lambda(diff_lam[l], lam_init)

        ka_all = jnp.concatenate([ka_c, ka], axis=1)
        va_all = jnp.concatenate([va_c, va], axis=1)
        kb_all = jnp.concatenate([kb_c, kb], axis=1)
        vb_all = jnp.concatenate([vb_c, vb], axis=1)
        ya = sweep_query_blocks(lambda qblk: gqa_attend(qblk, ka_all, va_all), qa)
        yb = sweep_query_blocks(lambda qblk: diff_attend(qblk, kb_all, vb_all, lam), qb)
        yb = rmsnorm(yb, diff_subln_g[l]) * (1.0 - lam_init)
        yc = fourier_mix(uc)
        h = h + ml[5][:, None, :] * merge_branches(ya, yb, yc, gl, w_branch[l], w_out[l])

        if not last:
            ya_c = gqa_attend(qa_c, ka_c, va_c)
            yb_c = rmsnorm(diff_attend(qb_c, kb_c, vb_c, lam), diff_subln_g[l]) * (1.0 - lam_init)
            yc_c = fourier_mix(uc_c)
            hc = hc + mc[5][:, None, :] * merge_branches(ya_c, yb_c, yc_c, gl_c, w_branch[l], w_out[l])

        h = ffn_half(h, ml, 2, norm_g[l, 2], ffn_wi[l, 1], ffn_wo[l, 1])
        if not last:
            hc = ffn_half(hc, mc, 2, norm_g[l, 2], ffn_wi[l, 1], ffn_wo[l, 1])
    return h
```

```python
import functools
import math

import jax
import jax.numpy as jnp
from jax import lax
from jax.experimental import pallas as pl
from jax.experimental.pallas import tpu as pltpu

F32 = jnp.float32
BF16 = jnp.bfloat16

D_MODEL = 1024
BATCH = 2
SEQ = 8192
DEPTH = 2
GRID_W = 64
CTX_LEN = 256
HEAD_DIM = 64
D_FF = 2816
N_MOD = 9
ROPE_THETA = 10000.0
EPS = 1e-6

LANES = 128
N_LAT = BATCH * SEQ
N_CTX = BATCH * CTX_LEN
R_ALL = N_LAT + N_CTX
TM = 512
NT_LAT = N_LAT // TM
NT_ALL = R_ALL // TM
TILES_PER_BATCH = SEQ // TM
FF_CHUNK = 256
N_FF = D_FF // FF_CHUNK
TQ = 256
NQ = SEQ // TQ
TK = 1024
ADA_TN = 1152
FFT_N1 = 64
FFT_N2 = 128
FFT_TN = 4096
FFT_KB = 16
VMEM_LIMIT = 56 * 1024 * 1024


def _cparams(n_axes):
    return pltpu.CompilerParams(dimension_semantics=("arbitrary",) * n_axes,
                                vmem_limit_bytes=VMEM_LIMIT)


def _resident(shape):
    nd = len(shape)
    return pl.BlockSpec(shape, lambda *_: (0,) * nd, pipeline_mode=pl.Buffered(1))


def _mod_set(i):
    return jnp.minimum(i // TILES_PER_BATCH, 2)


def _rms_mod(h, g, shift, scale):
    y = h * lax.rsqrt(jnp.mean(h * h, axis=-1, keepdims=True) + EPS) * g
    return y * (1.0 + scale) + shift


def _ada_kernel(ct_ref, w_ref, b_ref, o_ref):
    ct = ct_ref[...]
    s = ct * jax.nn.sigmoid(ct)
    w = w_ref[0]
    rows = [jnp.sum(w * s[:, r:r + 1], axis=0, keepdims=True) for r in range(3)]
    rows.append(jnp.zeros((5, ADA_TN), F32))
    o_ref[0] = jnp.concatenate(rows, axis=0) + b_ref[0]


def _ada_call(ct, w_ada, b_ada):
    n_col = N_MOD * D_MODEL
    return pl.pallas_call(
        _ada_kernel,
        grid=(DEPTH, n_col // ADA_TN),
        in_specs=[pl.BlockSpec((D_MODEL, 8), lambda l, j: (0, 0)),
                  pl.BlockSpec((1, D_MODEL, ADA_TN), lambda l, j: (l, 0, j)),
                  pl.BlockSpec((1, 1, ADA_TN), lambda l, j: (l, 0, j))],
        out_specs=pl.BlockSpec((1, 8, ADA_TN), lambda l, j: (l, 0, j)),
        out_shape=jax.ShapeDtypeStruct((DEPTH, 8, n_col), F32),
        compiler_params=_cparams(2),
        name="ada",
    )(ct, w_ada, b_ada.reshape(DEPTH, 1, n_col))


def _ffn_kernel(h_ref, mods_ref, g_ref, wi_ref, wo_ref, o_ref, acc_ref, *, k):
    h = h_ref[...]
    m = mods_ref[0]
    hn = _rms_mod(h, g_ref[...], m[3 * k:3 * k + 1], m[3 * k + 1:3 * k + 2]).astype(BF16)
    acc_ref[...] = jnp.zeros_like(acc_ref)

    def body(j, carry):
        gu = jnp.dot(hn, wi_ref[j], preferred_element_type=F32)
        g = gu[:, :FF_CHUNK]
        a = (g * jax.nn.sigmoid(g) * gu[:, FF_CHUNK:]).astype(BF16)
        acc_ref[...] += jnp.dot(a, wo_ref[j], preferred_element_type=F32)
        return carry

    lax.fori_loop(0, N_FF, body, 0)
    o_ref[...] = h + 0.5 * m[3 * k + 2:3 * k + 3] * acc_ref[...]


def _ffn_call(h, mods, g, wi, wo, *, k, n_tiles):
    rows = n_tiles * TM
    return pl.pallas_call(
        functools.partial(_ffn_kernel, k=k),
        grid=(n_tiles,),
        in_specs=[pl.BlockSpec((TM, D_MODEL), lambda i: (i, 0)),
                  pl.BlockSpec((1, N_MOD, D_MODEL), lambda i: (_mod_set(i), 0, 0)),
                  _resident((1, D_MODEL)),
                  _resident((N_FF, D_MODEL, 2 * FF_CHUNK)),
                  _resident((N_FF, FF_CHUNK, D_MODEL))],
        out_specs=pl.BlockSpec((TM, D_MODEL), lambda i: (i, 0)),
        out_shape=jax.ShapeDtypeStruct((rows, D_MODEL), F32),
        scratch_shapes=[pltpu.VMEM((TM, D_MODEL), F32)],
        compiler_params=_cparams(1),
        name=f"ffn{k}",
    )(h, mods, g, wi, wo)


N_PROJ = 4096


def _proj_kernel(h_ref, mods_ref, g_ref, w_ref, gt_ref, rc_ref, rs_ref,
                 qa_ref, ka_ref, va_ref, qb_ref, kb_ref, vb_ref, uc_ref):
    h = h_ref[...]
    m = mods_ref[0]
    hn = _rms_mod(h, g_ref[...], m[3:4], m[4:5]).astype(BF16)
    rc = rc_ref[...]
    rs = rs_ref[...]
    gt = gt_ref[...]
    lane = lax.broadcasted_iota(jnp.int32, (TM, LANES), 1)
    lo = lane < HEAD_DIM
    first_half = (lane % HEAD_DIM) < HEAD_DIM // 2
    inv_hd = 1.0 / HEAD_DIM

    def rope(y):
        swapped = jnp.where(first_half, pltpu.roll(y, 96, 1), pltpu.roll(y, 32, 1))
        return y * rc + swapped * rs

    def norm_one(z, g):
        ss = jnp.sum(z * z, axis=-1, keepdims=True) * inv_hd
        return rope(z * lax.rsqrt(ss + EPS) * g)

    def norm_two(z, g):
        zz = z * z
        s_lo = jnp.sum(jnp.where(lo, zz, 0.0), axis=-1, keepdims=True) * inv_hd
        s_hi = jnp.sum(jnp.where(lo, 0.0, zz), axis=-1, keepdims=True) * inv_hd
        r = jnp.where(lo, lax.rsqrt(s_lo + EPS), lax.rsqrt(s_hi + EPS))
        return rope(z * r * g)

    def chunk(c):
        return jnp.dot(hn, w_ref[:, 512 * c:512 * (c + 1)], preferred_element_type=F32)

    def slabs(z):
        return [z[:, LANES * j:LANES * (j + 1)] for j in range(4)]

    for c in range(2):
        z = chunk(c)
        qa_ref[:, 512 * c:512 * (c + 1)] = jnp.concatenate(
            [norm_one(s, gt[0:1]) for s in slabs(z)], axis=1).astype(BF16)
    s = slabs(chunk(2))
    ka_ref[...] = jnp.concatenate([norm_one(s[0], gt[1:2]), norm_one(s[1], gt[1:2])], axis=1).astype(BF16)
    ones_hi = jnp.where(lo, 0.0, 1.0)
    va_ref[...] = jnp.concatenate([s[2] + ones_hi, s[3] + ones_hi], axis=1).astype(BF16)
    qb_ref[...] = jnp.concatenate([norm_two(s, gt[2:3]) for s in slabs(chunk(3))], axis=1).astype(BF16)
    kb_ref[...] = jnp.concatenate([norm_two(s, gt[3:4]) for s in slabs(chunk(4))], axis=1).astype(BF16)
    for c in range(2):
        s = slabs(chunk(5 + c))
        vb_ref[:, 512 * c:512 * (c + 1)] = jnp.concatenate(
            [s[0], s[1] + 1.0, s[2], s[3] + 1.0], axis=1).astype(BF16)
    uc_ref[...] = chunk(7).astype(BF16)


def _proj_call(h, mods, g, w, gt, rc, rs):
    widths = (1024, 256, 256, 512, 512, 1024, 512)
    return pl.pallas_call(
        _proj_kernel,
        grid=(NT_ALL,),
        in_specs=[pl.BlockSpec((TM, D_MODEL), lambda i: (i, 0)),
                  pl.BlockSpec((1, N_MOD, D_MODEL), lambda i: (_mod_set(i), 0, 0)),
                  _resident((1, D_MODEL)),
                  _resident((D_MODEL, N_PROJ)),
                  _resident((4, LANES)),
                  pl.BlockSpec((TM, LANES), lambda i: (jnp.where(i < NT_LAT, i % TILES_PER_BATCH, TILES_PER_BATCH), 0)),
                  pl.BlockSpec((TM, LANES), lambda i: (jnp.where(i < NT_LAT, i % TILES_PER_BATCH, TILES_PER_BATCH), 0))],
        out_specs=[pl.BlockSpec((TM, wd), lambda i: (i, 0)) for wd in widths],
        out_shape=[jax.ShapeDtypeStruct((R_ALL, wd), BF16) for wd in widths],
        compiler_params=_cparams(1),
        name="proj",
    )(h, mods, g, w, gt, rc, rs)


def _flash_update(q, k, v, m_ref, acc_ref):
    s = lax.dot_general(q, k, (((1,), (1,)), ((), ())), preferred_element_type=F32)
    m_old = m_ref[...]
    m_new = jnp.maximum(m_old, jnp.max(s, axis=-1, keepdims=True))
    alpha = jnp.exp(m_old - m_new)
    p = jnp.exp(s - m_new).astype(BF16)
    acc_ref[...] = alpha * acc_ref[...] + jnp.dot(p, v, preferred_element_type=F32)
    m_ref[...] = m_new


def _n_lat_chunks(n_q_lat):
    return jnp.where(pl.program_id(2) < n_q_lat, SEQ // TK, 0)


def _attn_a_kernel(q_ref, kc_ref, vc_ref, kl_ref, vl_ref, o_ref, m_ref, acc_ref):
    q4 = jnp.concatenate([q_ref[:, LANES * j:LANES * (j + 1)] for j in range(4)], axis=0)
    m_ref[...] = jnp.full_like(m_ref, -1e30)
    acc_ref[...] = jnp.zeros_like(acc_ref)
    _flash_update(q4, kc_ref[...], vc_ref[...], m_ref, acc_ref)

    def body(c, carry):
        rows = pl.ds(pl.multiple_of(c * TK, TK), TK)
        _flash_update(q4, kl_ref[rows, :], vl_ref[rows, :], m_ref, acc_ref)
        return carry

    lax.fori_loop(0, _n_lat_chunks(NQ), body, 0)
    acc = acc_ref[...]
    o = acc / pltpu.roll(acc, HEAD_DIM, 1)
    lo = lax.broadcasted_iota(jnp.int32, (TQ, LANES), 1) < HEAD_DIM
    pairs = [jnp.where(lo, o[2 * j * TQ:(2 * j + 1) * TQ], pltpu.roll(o[(2 * j + 1) * TQ:(2 * j + 2) * TQ], HEAD_DIM, 1))
             for j in range(2)]
    o_ref[...] = jnp.concatenate(pairs, axis=1).astype(BF16)


def _q_row_block(b, qi):
    return jnp.where(qi < NQ, b * NQ + qi, 2 * NQ + b)


def _attn_a_call(qa, ka, va, *, with_ctx):
    n_qt = NQ + (1 if with_ctx else 0)
    return pl.pallas_call(
        _attn_a_kernel,
        grid=(BATCH, 2, n_qt),
        in_specs=[pl.BlockSpec((TQ, 512), lambda b, g, qi: (_q_row_block(b, qi), g)),
                  pl.BlockSpec((CTX_LEN, LANES), lambda b, g, qi: (2 * NQ + b, g)),
                  pl.BlockSpec((CTX_LEN, LANES), lambda b, g, qi: (2 * NQ + b, g)),
                  pl.BlockSpec((SEQ, LANES), lambda b, g, qi: (b, g)),
                  pl.BlockSpec((SEQ, LANES), lambda b, g, qi: (b, g))],
        out_specs=pl.BlockSpec((TQ, 256), lambda b, g, qi: (_q_row_block(b, qi), g)),
        out_shape=jax.ShapeDtypeStruct((R_ALL, 512), BF16),
        scratch_shapes=[pltpu.VMEM((4 * TQ, 1), F32), pltpu.VMEM((4 * TQ, LANES), F32)],
        compiler_params=_cparams(3),
        name="attn_a",
    )(qa, ka, va, ka, va)


def _attn_b_kernel(q_ref, kc_ref, vc_ref, kl_ref, vl_ref, dl_ref, sg_ref, o_ref, m_ref, acc_ref, *, lam_init):
    lo = lax.broadcasted_iota(jnp.int32, (TQ, LANES), 1) < HEAD_DIM
    zero = jnp.zeros((TQ, LANES), BF16)
    q2 = []
    for hh in range(2):
        qh = q_ref[:, LANES * hh:LANES * (hh + 1)]
        q2.append(jnp.concatenate([jnp.where(lo, qh, zero), jnp.where(lo, zero, qh)], axis=0))
    m_ref[...] = jnp.full_like(m_ref, -1e30)
    acc_ref[...] = jnp.zeros_like(acc_ref)

    def update(k_ref, v_ref, rows):
        for hh in range(2):
            _flash_update(q2[hh], k_ref[rows, LANES * hh:LANES * (hh + 1)],
                          v_ref[rows, 256 * hh:256 * (hh + 1)], m_ref.at[hh], acc_ref.at[hh])

    update(kc_ref, vc_ref, slice(None))

    def body(c, carry):
        update(kl_ref, vl_ref, pl.ds(pl.multiple_of(c * TK, TK), TK))
        return carry

    lax.fori_loop(0, _n_lat_chunks(NQ), body, 0)

    lp = dl_ref[...]
    lam = (jnp.exp(jnp.sum(lp[0:1] * lp[1:2], axis=-1, keepdims=True))
           - jnp.exp(jnp.sum(lp[2:3] * lp[3:4], axis=-1, keepdims=True)) + lam_init)
    outs = []
    for hh in range(2):
        acc = acc_ref[hh]
        o = acc[:, :LANES] / acc[:, LANES:]
        d = o[:TQ] - lam * o[TQ:]
        y = d * lax.rsqrt(jnp.mean(d * d, axis=-1, keepdims=True) + EPS) * sg_ref[...]
        outs.append(y * (1.0 - lam_init))
    o_ref[...] = jnp.concatenate(outs, axis=1).astype(BF16)


def _attn_b_call(qb, kb, vb, diff_lam_l, subln_g, *, with_ctx, lam_init):
    n_qt = NQ + (1 if with_ctx else 0)
    return pl.pallas_call(
        functools.partial(_attn_b_kernel, lam_init=lam_init),
        grid=(BATCH, 2, n_qt),
        in_specs=[pl.BlockSpec((TQ, 256), lambda b, hp, qi: (_q_row_block(b, qi), hp)),
                  pl.BlockSpec((CTX_LEN, 256), lambda b, hp, qi: (2 * NQ + b, hp)),
                  pl.BlockSpec((CTX_LEN, 512), lambda b, hp, qi: (2 * NQ + b, hp)),
                  pl.BlockSpec((SEQ, 256), lambda b, hp, qi: (b, hp)),
                  pl.BlockSpec((SEQ, 512), lambda b, hp, qi: (b, hp)),
                  pl.BlockSpec((4, HEAD_DIM), lambda b, hp, qi: (0, 0)),
                  pl.BlockSpec((1, LANES), lambda b, hp, qi: (0, 0))],
        out_specs=pl.BlockSpec((TQ, 256), lambda b, hp, qi: (_q_row_block(b, qi), hp)),
        out_shape=jax.ShapeDtypeStruct((R_ALL, 512), BF16),
        scratch_shapes=[pltpu.VMEM((2, 2 * TQ, 1), F32), pltpu.VMEM((2, 2 * TQ, 256), F32)],
        compiler_params=_cparams(3),
        name="attn_b",
    )(qb, kb, vb, kb, vb, diff_lam_l, subln_g)


def _fft1_kernel(x_ref, f_ref, tc_ref, ts_ref, yr_ref, yi_ref):
    y = jnp.dot(f_ref[...], x_ref[...], preferred_element_type=F32)
    for t in range(FFT_TN // 512):
        c = jnp.concatenate([tc_ref[:, LANES * t:LANES * (t + 1)]] * 4, axis=1)
        s = jnp.concatenate([ts_ref[:, LANES * t:LANES * (t + 1)]] * 4, axis=1)
        a = y[:FFT_N2, 512 * t:512 * (t + 1)]
        b = y[FFT_N2:, 512 * t:512 * (t + 1)]
        yr_ref[:, 512 * t:512 * (t + 1)] = (a * c + b * s).astype(BF16)
        yi_ref[:, 512 * t:512 * (t + 1)] = (b * c - a * s).astype(BF16)


def _fft1_call(uc_view, f_stack, tc, ts):
    n_col = FFT_N1 * 512
    return pl.pallas_call(
        _fft1_kernel,
        grid=(BATCH, n_col // FFT_TN),
        in_specs=[pl.BlockSpec((FFT_N2, FFT_TN), lambda b, j: (b, j)),
                  pl.BlockSpec((2 * FFT_N2, FFT_N2), lambda b, j: (0, 0)),
                  pl.BlockSpec((FFT_N2, FFT_TN // 4), lambda b, j: (0, j)),
                  pl.BlockSpec((FFT_N2, FFT_TN // 4), lambda b, j: (0, j))],
        out_specs=[pl.BlockSpec((FFT_N2, FFT_TN), lambda b, j: (b, j))] * 2,
        out_shape=[jax.ShapeDtypeStruct((BATCH * FFT_N2, n_col), BF16)] * 2,
        compiler_params=_cparams(2),
        name="fft1",
    )(uc_view, f_stack, tc, ts)


def _fft2_kernel(yr_ref, yi_ref, g_ref, zr_ref, zi_ref):
    for t in range(FFT_KB):
        ys = jnp.concatenate([yr_ref[t], yi_ref[t]], axis=0)
        z = jnp.dot(g_ref[...], ys, preferred_element_type=F32)
        zr_ref[:, 512 * t:512 * (t + 1)] = z[:FFT_N1].astype(BF16)
        zi_ref[:, 512 * t:512 * (t + 1)] = z[FFT_N1:].astype(BF16)


def _fft2_call(yr, yi, g_stack):
    steps = FFT_N2 // FFT_KB
    view_rows = R_ALL * 512 // (FFT_N2 * 512)
    return pl.pallas_call(
        _fft2_kernel,
        grid=(BATCH, steps),
        in_specs=[pl.BlockSpec((FFT_KB, FFT_N1, 512), lambda b, j: (b * steps + j, 0, 0)),
                  pl.BlockSpec((FFT_KB, FFT_N1, 512), lambda b, j: (b * steps + j, 0, 0)),
                  pl.BlockSpec((2 * FFT_N1, 2 * FFT_N1), lambda b, j: (0, 0))],
        out_specs=[pl.BlockSpec((FFT_N1, FFT_KB * 512), lambda b, j: (b, j))] * 2,
        out_shape=[jax.ShapeDtypeStruct((view_rows, FFT_N2 * 512), BF16)] * 2,
        compiler_params=_cparams(2),
        name="fft2",
    )(yr.reshape(BATCH * FFT_N2, FFT_N1, 512), yi.reshape(BATCH * FFT_N2, FFT_N1, 512), g_stack)


def _fft_ctx_kernel(x_ref, f_ref, zr_in, zi_in, zr_ref, zi_ref):
    del zr_in, zi_in
    z = jnp.dot(f_ref[...], x_ref[...], preferred_element_type=F32)
    zr_ref[...] = z[:CTX_LEN].astype(BF16)
    zi_ref[...] = z[CTX_LEN:].astype(BF16)


def _fft_ctx_call(uc, f_stack, zr, zi):
    return pl.pallas_call(
        _fft_ctx_kernel,
        grid=(BATCH,),
        in_specs=[pl.BlockSpec((CTX_LEN, 512), lambda b: (2 * NQ + b, 0)),
                  pl.BlockSpec((2 * CTX_LEN, CTX_LEN), lambda b: (0, 0)),
                  pl.BlockSpec(memory_space=pl.ANY),
                  pl.BlockSpec(memory_space=pl.ANY)],
        out_specs=[pl.BlockSpec((CTX_LEN, 512), lambda b: (2 * NQ + b, 0))] * 2,
        out_shape=[jax.ShapeDtypeStruct((R_ALL, 512), BF16)] * 2,
        input_output_aliases={2: 0, 3: 1},
        compiler_params=_cparams(1),
        name="fft_ctx",
    )(uc, f_stack, zr, zi)


def _merge_kernel(h_ref, mods_ref, g_ref, ya_ref, yb_ref, zr_ref, zi_ref, wg_ref, wb_ref, wo_ref, cs_ref, o_ref):
    h = h_ref[...]
    m = mods_ref[0]
    hn = _rms_mod(h, g_ref[...], m[3:4], m[4:5]).astype(BF16)
    zr = zr_ref[...]
    zi = zi_ref[...]
    yc = jnp.concatenate(
        [jnp.dot(jnp.concatenate([zr[:, LANES * g:LANES * (g + 1)], zi[:, LANES * g:LANES * (g + 1)]], axis=1),
                 cs_ref[...], preferred_element_type=F32) for g in range(4)], axis=1).astype(BF16)
    branches = (ya_ref[...], yb_ref[...], yc)
    mix = jnp.zeros((TM, D_MODEL), F32)
    for i in range(3):
        gl = jnp.dot(hn, wg_ref[i], preferred_element_type=F32)
        pr = jnp.dot(branches[i], wb_ref[i], preferred_element_type=F32)
        mix = mix + jax.nn.sigmoid(gl) * pr
    o_ref[...] = h + m[5:6] * jnp.dot(mix.astype(BF16), wo_ref[...], preferred_element_type=F32)


def _merge_call(h, mods, g, ya, yb, zr, zi, wg, wb, wo, cs, *, n_tiles):
    row = lambda i: (i, 0)
    return pl.pallas_call(
        _merge_kernel,
        grid=(n_tiles,),
        in_specs=[pl.BlockSpec((TM, D_MODEL), row),
                  pl.BlockSpec((1, N_MOD, D_MODEL), lambda i: (_mod_set(i), 0, 0)),
                  _resident((1, D_MODEL)),
                  pl.BlockSpec((TM, 512), row), pl.BlockSpec((TM, 512), row),
                  pl.BlockSpec((TM, 512), row), pl.BlockSpec((TM, 512), row),
                  _resident((3, D_MODEL, D_MODEL)),
                  _resident((3, 512, D_MODEL)),
                  _resident((D_MODEL, D_MODEL)),
                  _resident((2 * LANES, LANES))],
        out_specs=pl.BlockSpec((TM, D_MODEL), row),
        out_shape=jax.ShapeDtypeStruct((n_tiles * TM, D_MODEL), F32),
        compiler_params=_cparams(1),
        name="merge",
    )(h, mods, g, ya, yb, zr, zi, wg, wb, wo, cs)


def _dft_cos_sin(n, rows=None, cols=None):
    j = jnp.arange(n if rows is None else rows, dtype=jnp.int32)[:, None]
    k = jnp.arange(n if cols is None else cols, dtype=jnp.int32)[None, :]
    ang = ((j * k) % n).astype(F32) * (2.0 * math.pi / n)
    return jnp.cos(ang), jnp.sin(ang)


def _rope_tables():
    rows = SEQ // GRID_W
    row = jnp.repeat(jnp.arange(rows, dtype=F32), GRID_W)
    col = jnp.tile(jnp.arange(GRID_W, dtype=F32), rows)
    n_freq = HEAD_DIM // 4
    inv = jnp.power(ROPE_THETA, -jnp.arange(n_freq, dtype=F32) / n_freq)
    ang = jnp.concatenate([row[:, None] * inv, col[:, None] * inv], axis=-1)
    cos, sin = jnp.cos(ang), jnp.sin(ang)
    rc = jnp.tile(cos, (1, 4))
    rs = jnp.tile(jnp.concatenate([-sin, sin], axis=-1), (1, 2))
    rc = jnp.concatenate([rc, jnp.ones((TM, LANES), F32)], axis=0)
    rs = jnp.concatenate([rs, jnp.zeros((TM, LANES), F32)], axis=0)
    return rc, rs


def _pad_cols(w, n_groups, width, padded):
    d = w.shape[0]
    w = w.reshape(d, n_groups, width)
    return jnp.pad(w, ((0, 0), (0, 0), (0, padded - width))).reshape(d, n_groups * padded)


def _proj_weight(w_in_l):
    o = 0
    parts = []
    for n_groups, width, padded in ((8, 64, 128), (2, 64, 128), (2, 64, 128), (1, 512, 512),
                                    (1, 512, 512), (4, 128, 256), (1, 512, 512)):
        n = n_groups * width
        parts.append(_pad_cols(w_in_l[:, o:o + n], n_groups, width, padded))
        o += n
    return jnp.concatenate(parts, axis=1).astype(BF16), o


def kernel(x, c, ctx, c_ctx, w_ada, b_ada, norm_g, ffn_wi, ffn_wo, w_in, qk_g, diff_lam, diff_subln_g, w_branch, w_out):
    rc, rs = _rope_tables()
    c128, s128 = _dft_cos_sin(FFT_N2)
    f1 = (jnp.concatenate([c128, -s128], axis=0) * FFT_N2 ** -0.5).astype(BF16)
    c64, s64 = _dft_cos_sin(FFT_N1)
    g2 = (jnp.concatenate([jnp.concatenate([c64, s64], axis=1),
                           jnp.concatenate([-s64, c64], axis=1)], axis=0) * FFT_N1 ** -0.5).astype(BF16)
    twc, tws = _dft_cos_sin(SEQ, rows=FFT_N2, cols=FFT_N1)
    twc = jnp.repeat(twc, LANES, axis=1)
    tws = jnp.repeat(tws, LANES, axis=1)
    cctx, sctx = _dft_cos_sin(CTX_LEN)
    fctx = (jnp.concatenate([cctx, -sctx], axis=0) * CTX_LEN ** -0.5).astype(BF16)
    cs = (jnp.concatenate([c128, s128], axis=0) * LANES ** -0.5).astype(BF16)

    ct = jnp.zeros((D_MODEL, 8), F32).at[:, 0:BATCH].set(c.T).at[:, BATCH].set(c_ctx)
    mods_all = _ada_call(ct, w_ada, b_ada).reshape(DEPTH, 8, N_MOD, D_MODEL)[:, :3]

    h = jnp.concatenate([x.reshape(N_LAT, D_MODEL), ctx.reshape(N_CTX, D_MODEL)], axis=0)
    q_scale = jnp.array([HEAD_DIM ** -0.5, 1.0, HEAD_DIM ** -0.5, 1.0], F32)[:, None]
    for l in range(DEPTH):
        last = l == DEPTH - 1
        mods = mods_all[l]
        ng = norm_g[l].reshape(3, 1, D_MODEL)
        wi = []
        wo = []
        for k in range(2):
            w = ffn_wi[l, k]
            gate = w[:, :D_FF].reshape(D_MODEL, N_FF, FF_CHUNK)
            up = w[:, D_FF:].reshape(D_MODEL, N_FF, FF_CHUNK)
            wi.append(jnp.concatenate([gate, up], axis=-1).transpose(1, 0, 2).astype(BF16))
            wo.append(ffn_wo[l, k].reshape(N_FF, FF_CHUNK, D_MODEL).astype(BF16))
        w_proj, gate_off = _proj_weight(w_in[l])
        w_gate = w_in[l][:, gate_off:].reshape(D_MODEL, 3, D_MODEL).transpose(1, 0, 2).astype(BF16)
        gt = jnp.tile(qk_g[l], (1, 2)) * q_scale
        lam_init = 0.8 - 0.6 * math.exp(-0.3 * l)

        h = _ffn_call(h, mods, ng[0], wi[0], wo[0], k=0, n_tiles=NT_ALL)
        qa, ka, va, qb, kb, vb, uc = _proj_call(h, mods, ng[1], w_proj, gt, rc, rs)
        ya = _attn_a_call(qa, ka, va, with_ctx=not last)
        yb = _attn_b_call(qb, kb, vb, diff_lam[l], jnp.tile(diff_subln_g[l][None, :], (1, 1)),
                          with_ctx=not last, lam_init=lam_init)
        yr, yi = _fft1_call(uc.reshape(R_ALL * 512 // (FFT_N1 * 512), FFT_N1 * 512), f1, twc, tws)
        zr, zi = _fft2_call(yr, yi, g2)
        zr = zr.reshape(R_ALL, 512)
        zi = zi.reshape(R_ALL, 512)
        if not last:
            zr, zi = _fft_ctx_call(uc, fctx, zr, zi)
        n_tiles = NT_LAT if last else NT_ALL
        h = _merge_call(h, mods, ng[1], ya, yb, zr, zi, w_gate, w_branch[l].astype(BF16),
                        w_out[l].astype(BF16), cs, n_tiles=n_tiles)
        h = _ffn_call(h, mods, ng[2], wi[1], wo[1], k=2, n_tiles=n_tiles)
    return h.reshape(BATCH, SEQ, D_MODEL)
```

```python
import functools
import math

import jax
import jax.numpy as jnp
from jax import lax
from jax.experimental import pallas as pl
from jax.experimental.pallas import tpu as pltpu

F32 = jnp.float32
BF16 = jnp.bfloat16

D_MODEL = 1024
BATCH = 2
SEQ = 8192
DEPTH = 2
GRID_W = 64
CTX_LEN = 256
HEAD_DIM = 64
D_FF = 2816
N_MOD = 9
ROPE_THETA = 10000.0
EPS = 1e-6

LANES = 128
N_LAT = BATCH * SEQ
N_CTX = BATCH * CTX_LEN
R_ALL = N_LAT + N_CTX
TM = 512
NT_LAT = N_LAT // TM
NT_ALL = R_ALL // TM
TILES_PER_BATCH = SEQ // TM
FF_CHUNK = 256
N_FF = D_FF // FF_CHUNK
TQ = 256
NQ = SEQ // TQ
TK = 1024
ADA_TN = 1152
FFT_N1 = 64
FFT_N2 = 128
FFT_TN = 4096
FFT_KB = 16
VMEM_LIMIT = 56 * 1024 * 1024


def _cparams(n_axes):
    return pltpu.CompilerParams(dimension_semantics=("arbitrary",) * n_axes,
                                vmem_limit_bytes=VMEM_LIMIT)


def _resident(shape):
    nd = len(shape)
    return pl.BlockSpec(shape, lambda *_: (0,) * nd, pipeline_mode=pl.Buffered(1))


def _mod_set(i):
    return jnp.minimum(i // TILES_PER_BATCH, 2)


def _rms_mod(h, g, shift, scale):
    y = h * lax.rsqrt(jnp.mean(h * h, axis=-1, keepdims=True) + EPS) * g
    return y * (1.0 + scale) + shift


def _ada_kernel(ct_ref, w_ref, b_ref, o_ref):
    ct = ct_ref[...]
    s = ct * jax.nn.sigmoid(ct)
    w = w_ref[0]
    rows = [jnp.sum(w * s[:, r:r + 1], axis=0, keepdims=True) for r in range(3)]
    rows.append(jnp.zeros((5, ADA_TN), F32))
    o_ref[0] = jnp.concatenate(rows, axis=0) + b_ref[0]


def _ada_call(ct, w_ada, b_ada):
    n_col = N_MOD * D_MODEL
    return pl.pallas_call(
        _ada_kernel,
        grid=(DEPTH, n_col // ADA_TN),
        in_specs=[pl.BlockSpec((D_MODEL, 8), lambda l, j: (0, 0)),
                  pl.BlockSpec((1, D_MODEL, ADA_TN), lambda l, j: (l, 0, j)),
                  pl.BlockSpec((1, 1, ADA_TN), lambda l, j: (l, 0, j))],
        out_specs=pl.BlockSpec((1, 8, ADA_TN), lambda l, j: (l, 0, j)),
        out_shape=jax.ShapeDtypeStruct((DEPTH, 8, n_col), F32),
        compiler_params=_cparams(2),
        name="ada",
    )(ct, w_ada, b_ada.reshape(DEPTH, 1, n_col))


def _ffn_kernel(h_ref, mods_ref, g_ref, wi_ref, wo_ref, o_ref, acc_ref, *, k):
    h = h_ref[...]
    m = mods_ref[0]
    hn = _rms_mod(h, g_ref[...], m[3 * k:3 * k + 1], m[3 * k + 1:3 * k + 2]).astype(BF16)
    acc_ref[...] = jnp.zeros_like(acc_ref)

    def body(j, carry):
        gu = jnp.dot(hn, wi_ref[j], preferred_element_type=F32)
        g = gu[:, :FF_CHUNK]
        a = (g * jax.nn.sigmoid(g) * gu[:, FF_CHUNK:]).astype(BF16)
        acc_ref[...] += jnp.dot(a, wo_ref[j], preferred_element_type=F32)
        return carry

    lax.fori_loop(0, N_FF, body, 0)
    o_ref[...] = h + 0.5 * m[3 * k + 2:3 * k + 3] * acc_ref[...]


def _ffn_call(h, mods, g, wi, wo, *, k, n_tiles):
    rows = n_tiles * TM
    return pl.pallas_call(
        functools.partial(_ffn_kernel, k=k),
        grid=(n_tiles,),
        in_specs=[pl.BlockSpec((TM, D_MODEL), lambda i: (i, 0)),
                  pl.BlockSpec((1, N_MOD, D_MODEL), lambda i: (_mod_set(i), 0, 0)),
                  _resident((1, D_MODEL)),
                  _resident((N_FF, D_MODEL, 2 * FF_CHUNK)),
                  _resident((N_FF, FF_CHUNK, D_MODEL))],
        out_specs=pl.BlockSpec((TM, D_MODEL), lambda i: (i, 0)),
        out_shape=jax.ShapeDtypeStruct((rows, D_MODEL), F32),
        scratch_shapes=[pltpu.VMEM((TM, D_MODEL), F32)],
        compiler_params=_cparams(1),
        name=f"ffn{k}",
    )(h, mods, g, wi, wo)


N_PROJ = 4096


def _proj_kernel(h_ref, mods_ref, g_ref, w_ref, gt_ref, rc_ref, rs_ref,
                 qa_ref, ka_ref, va_ref, qb_ref, kb_ref, vb_ref, uc_ref):
    h = h_ref[...]
    m = mods_ref[0]
    hn = _rms_mod(h, g_ref[...], m[3:4], m[4:5]).astype(BF16)
    rc = rc_ref[...]
    rs = rs_ref[...]
    gt = gt_ref[...]
    lane = lax.broadcasted_iota(jnp.int32, (TM, LANES), 1)
    lo = lane < HEAD_DIM
    first_half = (lane % HEAD_DIM) < HEAD_DIM // 2
    inv_hd = 1.0 / HEAD_DIM

    def rope(y):
        swapped = jnp.where(first_half, pltpu.roll(y, 96, 1), pltpu.roll(y, 32, 1))
        return y * rc + swapped * rs

    def norm_one(z, g):
        ss = jnp.sum(z * z, axis=-1, keepdims=True) * inv_hd
        return rope(z * lax.rsqrt(ss + EPS) * g)

    def norm_two(z, g):
        zz = z * z
        s_lo = jnp.sum(jnp.where(lo, zz, 0.0), axis=-1, keepdims=True) * inv_hd
        s_hi = jnp.sum(jnp.where(lo, 0.0, zz), axis=-1, keepdims=True) * inv_hd
        r = jnp.where(lo, lax.rsqrt(s_lo + EPS), lax.rsqrt(s_hi + EPS))
        return rope(z * r * g)

    def chunk(c):
        return jnp.dot(hn, w_ref[:, 512 * c:512 * (c + 1)], preferred_element_type=F32)

    def slabs(z):
        return [z[:, LANES * j:LANES * (j + 1)] for j in range(4)]

    for c in range(2):
        z = chunk(c)
        qa_ref[:, 512 * c:512 * (c + 1)] = jnp.concatenate(
            [norm_one(s, gt[0:1]) for s in slabs(z)], axis=1).astype(BF16)
    s = slabs(chunk(2))
    ka_ref[...] = jnp.concatenate([norm_one(s[0], gt[1:2]), norm_one(s[1], gt[1:2])], axis=1).astype(BF16)
    ones_hi = jnp.where(lo, 0.0, 1.0)
    va_ref[...] = jnp.concatenate([s[2] + ones_hi, s[3] + ones_hi], axis=1).astype(BF16)
    qb_ref[...] = jnp.concatenate([norm_two(s, gt[2:3]) for s in slabs(chunk(3))], axis=1).astype(BF16)
    kb_ref[...] = jnp.concatenate([norm_two(s, gt[3:4]) for s in slabs(chunk(4))], axis=1).astype(BF16)
    for c in range(2):
        s = slabs(chunk(5 + c))
        vb_ref[:, 512 * c:512 * (c + 1)] = jnp.concatenate(
            [s[0], s[1] + 1.0, s[2], s[3] + 1.0], axis=1).astype(BF16)
    uc_ref[...] = chunk(7).astype(BF16)


def _proj_call(h, mods, g, w, gt, rc, rs):
    widths = (1024, 256, 256, 512, 512, 1024, 512)
    return pl.pallas_call(
        _proj_kernel,
        grid=(NT_ALL,),
        in_specs=[pl.BlockSpec((TM, D_MODEL), lambda i: (i, 0)),
                  pl.BlockSpec((1, N_MOD, D_MODEL), lambda i: (_mod_set(i), 0, 0)),
                  _resident((1, D_MODEL)),
                  _resident((D_MODEL, N_PROJ)),
                  _resident((4, LANES)),
                  pl.BlockSpec((TM, LANES), lambda i: (jnp.where(i < NT_LAT, i % TILES_PER_BATCH, TILES_PER_BATCH), 0)),
                  pl.BlockSpec((TM, LANES), lambda i: (jnp.where(i < NT_LAT, i % TILES_PER_BATCH, TILES_PER_BATCH), 0))],
        out_specs=[pl.BlockSpec((TM, wd), lambda i: (i, 0)) for wd in widths],
        out_shape=[jax.ShapeDtypeStruct((R_ALL, wd), BF16) for wd in widths],
        compiler_params=_cparams(1),
        name="proj",
    )(h, mods, g, w, gt, rc, rs)


N_LAT_CHUNKS = SEQ // TK
CTX_CHUNK = -1


def _attn_pipeline(chunks, qs, k_of, v_of, s_ref, p_ref, al_ref, m_ref, acc_ref):
    width = lambda c: CTX_LEN if c == CTX_CHUNK else TK

    def qk(i):
        for st, q in enumerate(qs):
            s_ref[st, i % 2, :, :width(chunks[i])] = lax.dot_general(
                q, k_of(chunks[i], st), (((1,), (1,)), ((), ())), preferred_element_type=F32)

    def softmax(i):
        for st in range(len(qs)):
            s = s_ref[st, i % 2, :, :width(chunks[i])]
            m_old = m_ref[st]
            m_new = jnp.maximum(m_old, jnp.max(s, axis=-1, keepdims=True))
            al_ref[st, i % 2] = jnp.exp(m_old - m_new)
            p_ref[st, i % 2, :, :width(chunks[i])] = jnp.exp(s - m_new).astype(BF16)
            m_ref[st] = m_new

    def pv(i):
        for st in range(len(qs)):
            acc_ref[st] = al_ref[st, i % 2] * acc_ref[st] + jnp.dot(
                p_ref[st, i % 2, :, :width(chunks[i])], v_of(chunks[i], st), preferred_element_type=F32)

    m_ref[...] = jnp.full_like(m_ref, -1e30)
    acc_ref[...] = jnp.zeros_like(acc_ref)
    n = len(chunks)
    qk(0)
    if n > 1:
        qk(1)
    softmax(0)
    for i in range(n - 2):
        qk(i + 2)
        softmax(i + 1)
        pv(i)
    if n > 1:
        softmax(n - 1)
        pv(n - 2)
    pv(n - 1)


def _run_chunks(with_ctx, run):
    all_chunks = [CTX_CHUNK] + list(range(N_LAT_CHUNKS))
    if not with_ctx:
        run(all_chunks)
        return
    qi = pl.program_id(2)

    @pl.when(qi < NQ)
    def _():
        run(all_chunks)

    @pl.when(qi == NQ)
    def _():
        run([CTX_CHUNK])


def _chunk_rows(c):
    return slice(None) if c == CTX_CHUNK else pl.ds(c * TK, TK)


def _attn_a_kernel(q_ref, kc_ref, vc_ref, kl_ref, vl_ref, o_ref, s_ref, p_ref, al_ref, m_ref, acc_ref, *, with_ctx):
    q4 = jnp.concatenate([q_ref[:, LANES * j:LANES * (j + 1)] for j in range(4)], axis=0)
    k_of = lambda c, st: (kc_ref if c == CTX_CHUNK else kl_ref)[_chunk_rows(c), :]
    v_of = lambda c, st: (vc_ref if c == CTX_CHUNK else vl_ref)[_chunk_rows(c), :]
    _run_chunks(with_ctx, lambda chunks: _attn_pipeline(chunks, [q4], k_of, v_of, s_ref, p_ref, al_ref, m_ref, acc_ref))
    acc = acc_ref[0]
    o = acc / pltpu.roll(acc, HEAD_DIM, 1)
    lo = lax.broadcasted_iota(jnp.int32, (TQ, LANES), 1) < HEAD_DIM
    pairs = [jnp.where(lo, o[2 * j * TQ:(2 * j + 1) * TQ], pltpu.roll(o[(2 * j + 1) * TQ:(2 * j + 2) * TQ], HEAD_DIM, 1))
             for j in range(2)]
    o_ref[...] = jnp.concatenate(pairs, axis=1).astype(BF16)


def _q_row_block(b, qi):
    return jnp.where(qi < NQ, b * NQ + qi, 2 * NQ + b)


def _attn_scratch(n_streams, rows, n_v):
    return [pltpu.VMEM((n_streams, 2, rows, TK), F32), pltpu.VMEM((n_streams, 2, rows, TK), BF16),
            pltpu.VMEM((n_streams, 2, rows, 1), F32), pltpu.VMEM((n_streams, rows, 1), F32),
            pltpu.VMEM((n_streams, rows, n_v), F32)]


def _attn_a_call(qa, ka, va, *, with_ctx):
    n_qt = NQ + (1 if with_ctx else 0)
    return pl.pallas_call(
        functools.partial(_attn_a_kernel, with_ctx=with_ctx),
        grid=(BATCH, 2, n_qt),
        in_specs=[pl.BlockSpec((TQ, 512), lambda b, g, qi: (_q_row_block(b, qi), g)),
                  pl.BlockSpec((CTX_LEN, LANES), lambda b, g, qi: (2 * NQ + b, g)),
                  pl.BlockSpec((CTX_LEN, LANES), lambda b, g, qi: (2 * NQ + b, g)),
                  pl.BlockSpec((SEQ, LANES), lambda b, g, qi: (b, g), pipeline_mode=pl.Buffered(1)),
                  pl.BlockSpec((SEQ, LANES), lambda b, g, qi: (b, g), pipeline_mode=pl.Buffered(1))],
        out_specs=pl.BlockSpec((TQ, 256), lambda b, g, qi: (_q_row_block(b, qi), g)),
        out_shape=jax.ShapeDtypeStruct((R_ALL, 512), BF16),
        scratch_shapes=_attn_scratch(1, 4 * TQ, LANES),
        compiler_params=_cparams(3),
        name="attn_a",
    )(qa, ka, va, ka, va)


def _attn_b_kernel(q_ref, kc_ref, vc_ref, kl_ref, vl_ref, dl_ref, sg_ref, o_ref,
                   s_ref, p_ref, al_ref, m_ref, acc_ref, *, with_ctx, lam_init):
    lo = lax.broadcasted_iota(jnp.int32, (TQ, LANES), 1) < HEAD_DIM
    zero = jnp.zeros((TQ, LANES), BF16)
    q2 = []
    for hh in range(2):
        qh = q_ref[:, LANES * hh:LANES * (hh + 1)]
        q2.append(jnp.concatenate([jnp.where(lo, qh, zero), jnp.where(lo, zero, qh)], axis=0))
    k_of = lambda c, hh: (kc_ref if c == CTX_CHUNK else kl_ref)[_chunk_rows(c), LANES * hh:LANES * (hh + 1)]
    v_of = lambda c, hh: (vc_ref if c == CTX_CHUNK else vl_ref)[_chunk_rows(c), 256 * hh:256 * (hh + 1)]
    _run_chunks(with_ctx, lambda chunks: _attn_pipeline(chunks, q2, k_of, v_of, s_ref, p_ref, al_ref, m_ref, acc_ref))

    lp = dl_ref[...]
    lam = (jnp.exp(jnp.sum(lp[0:1] * lp[1:2], axis=-1, keepdims=True))
           - jnp.exp(jnp.sum(lp[2:3] * lp[3:4], axis=-1, keepdims=True)) + lam_init)
    outs = []
    for hh in range(2):
        acc = acc_ref[hh]
        o = acc[:, :LANES] / acc[:, LANES:]
        d = o[:TQ] - lam * o[TQ:]
        y = d * lax.rsqrt(jnp.mean(d * d, axis=-1, keepdims=True) + EPS) * sg_ref[...]
        outs.append(y * (1.0 - lam_init))
    o_ref[...] = jnp.concatenate(outs, axis=1).astype(BF16)


def _attn_b_call(qb, kb, vb, diff_lam_l, subln_g, *, with_ctx, lam_init):
    n_qt = NQ + (1 if with_ctx else 0)
    return pl.pallas_call(
        functools.partial(_attn_b_kernel, with_ctx=with_ctx, lam_init=lam_init),
        grid=(BATCH, 2, n_qt),
        in_specs=[pl.BlockSpec((TQ, 256), lambda b, hp, qi: (_q_row_block(b, qi), hp)),
                  pl.BlockSpec((CTX_LEN, 256), lambda b, hp, qi: (2 * NQ + b, hp)),
                  pl.BlockSpec((CTX_LEN, 512), lambda b, hp, qi: (2 * NQ + b, hp)),
                  pl.BlockSpec((SEQ, 256), lambda b, hp, qi: (b, hp), pipeline_mode=pl.Buffered(1)),
                  pl.BlockSpec((SEQ, 512), lambda b, hp, qi: (b, hp), pipeline_mode=pl.Buffered(1)),
                  pl.BlockSpec((4, HEAD_DIM), lambda b, hp, qi: (0, 0)),
                  pl.BlockSpec((1, LANES), lambda b, hp, qi: (0, 0))],
        out_specs=pl.BlockSpec((TQ, 256), lambda b, hp, qi: (_q_row_block(b, qi), hp)),
        out_shape=jax.ShapeDtypeStruct((R_ALL, 512), BF16),
        scratch_shapes=_attn_scratch(2, 2 * TQ, 256),
        compiler_params=_cparams(3),
        name="attn_b",
    )(qb, kb, vb, kb, vb, diff_lam_l, subln_g)


def _fft1_kernel(x_ref, f_ref, tc_ref, ts_ref, yr_ref, yi_ref):
    y = jnp.dot(f_ref[...], x_ref[...], preferred_element_type=F32)
    for t in range(FFT_TN // 512):
        c = jnp.concatenate([tc_ref[:, LANES * t:LANES * (t + 1)]] * 4, axis=1)
        s = jnp.concatenate([ts_ref[:, LANES * t:LANES * (t + 1)]] * 4, axis=1)
        a = y[:FFT_N2, 512 * t:512 * (t + 1)]
        b = y[FFT_N2:, 512 * t:512 * (t + 1)]
        yr_ref[:, 512 * t:512 * (t + 1)] = (a * c + b * s).astype(BF16)
        yi_ref[:, 512 * t:512 * (t + 1)] = (b * c - a * s).astype(BF16)


def _fft1_call(uc_view, f_stack, tc, ts):
    n_col = FFT_N1 * 512
    return pl.pallas_call(
        _fft1_kernel,
        grid=(BATCH, n_col // FFT_TN),
        in_specs=[pl.BlockSpec((FFT_N2, FFT_TN), lambda b, j: (b, j)),
                  pl.BlockSpec((2 * FFT_N2, FFT_N2), lambda b, j: (0, 0)),
                  pl.BlockSpec((FFT_N2, FFT_TN // 4), lambda b, j: (0, j)),
                  pl.BlockSpec((FFT_N2, FFT_TN // 4), lambda b, j: (0, j))],
        out_specs=[pl.BlockSpec((FFT_N2, FFT_TN), lambda b, j: (b, j))] * 2,
        out_shape=[jax.ShapeDtypeStruct((BATCH * FFT_N2, n_col), BF16)] * 2,
        compiler_params=_cparams(2),
        name="fft1",
    )(uc_view, f_stack, tc, ts)


def _fft2_kernel(yr_ref, yi_ref, g_ref, zr_ref, zi_ref):
    for t in range(FFT_KB):
        ys = jnp.concatenate([yr_ref[t], yi_ref[t]], axis=0)
        z = jnp.dot(g_ref[...], ys, preferred_element_type=F32)
        zr_ref[:, 512 * t:512 * (t + 1)] = z[:FFT_N1].astype(BF16)
        zi_ref[:, 512 * t:512 * (t + 1)] = z[FFT_N1:].astype(BF16)


def _fft2_call(yr, yi, g_stack):
    steps = FFT_N2 // FFT_KB
    view_rows = R_ALL * 512 // (FFT_N2 * 512)
    return pl.pallas_call(
        _fft2_kernel,
        grid=(BATCH, steps),
        in_specs=[pl.BlockSpec((FFT_KB, FFT_N1, 512), lambda b, j: (b * steps + j, 0, 0)),
                  pl.BlockSpec((FFT_KB, FFT_N1, 512), lambda b, j: (b * steps + j, 0, 0)),
                  pl.BlockSpec((2 * FFT_N1, 2 * FFT_N1), lambda b, j: (0, 0))],
        out_specs=[pl.BlockSpec((FFT_N1, FFT_KB * 512), lambda b, j: (b, j))] * 2,
        out_shape=[jax.ShapeDtypeStruct((view_rows, FFT_N2 * 512), BF16)] * 2,
        compiler_params=_cparams(2),
        name="fft2",
    )(yr.reshape(BATCH * FFT_N2, FFT_N1, 512), yi.reshape(BATCH * FFT_N2, FFT_N1, 512), g_stack)


def _fft_ctx_kernel(x_ref, f_ref, zr_in, zi_in, zr_ref, zi_ref):
    del zr_in, zi_in
    z = jnp.dot(f_ref[...], x_ref[...], preferred_element_type=F32)
    zr_ref[...] = z[:CTX_LEN].astype(BF16)
    zi_ref[...] = z[CTX_LEN:].astype(BF16)


def _fft_ctx_call(uc, f_stack, zr, zi):
    return pl.pallas_call(
        _fft_ctx_kernel,
        grid=(BATCH,),
        in_specs=[pl.BlockSpec((CTX_LEN, 512), lambda b: (2 * NQ + b, 0)),
                  pl.BlockSpec((2 * CTX_LEN, CTX_LEN), lambda b: (0, 0)),
                  pl.BlockSpec(memory_space=pl.ANY),
                  pl.BlockSpec(memory_space=pl.ANY)],
        out_specs=[pl.BlockSpec((CTX_LEN, 512), lambda b: (2 * NQ + b, 0))] * 2,
        out_shape=[jax.ShapeDtypeStruct((R_ALL, 512), BF16)] * 2,
        input_output_aliases={2: 0, 3: 1},
        compiler_params=_cparams(1),
        name="fft_ctx",
    )(uc, f_stack, zr, zi)


def _merge_kernel(h_ref, mods_ref, g_ref, ya_ref, yb_ref, zr_ref, zi_ref, wg_ref, wb_ref, wo_ref, cs_ref, o_ref):
    h = h_ref[...]
    m = mods_ref[0]
    hn = _rms_mod(h, g_ref[...], m[3:4], m[4:5]).astype(BF16)
    zr = zr_ref[...]
    zi = zi_ref[...]
    yc = jnp.concatenate(
        [jnp.dot(jnp.concatenate([zr[:, LANES * g:LANES * (g + 1)], zi[:, LANES * g:LANES * (g + 1)]], axis=1),
                 cs_ref[...], preferred_element_type=F32) for g in range(4)], axis=1).astype(BF16)
    branches = (ya_ref[...], yb_ref[...], yc)
    mix = jnp.zeros((TM, D_MODEL), F32)
    for i in range(3):
        gl = jnp.dot(hn, wg_ref[i], preferred_element_type=F32)
        pr = jnp.dot(branches[i], wb_ref[i], preferred_element_type=F32)
        mix = mix + jax.nn.sigmoid(gl) * pr
    o_ref[...] = h + m[5:6] * jnp.dot(mix.astype(BF16), wo_ref[...], preferred_element_type=F32)


def _merge_call(h, mods, g, ya, yb, zr, zi, wg, wb, wo, cs, *, n_tiles):
    row = lambda i: (i, 0)
    return pl.pallas_call(
        _merge_kernel,
        grid=(n_tiles,),
        in_specs=[pl.BlockSpec((TM, D_MODEL), row),
                  pl.BlockSpec((1, N_MOD, D_MODEL), lambda i: (_mod_set(i), 0, 0)),
                  _resident((1, D_MODEL)),
                  pl.BlockSpec((TM, 512), row), pl.BlockSpec((TM, 512), row),
                  pl.BlockSpec((TM, 512), row), pl.BlockSpec((TM, 512), row),
                  _resident((3, D_MODEL, D_MODEL)),
                  _resident((3, 512, D_MODEL)),
                  _resident((D_MODEL, D_MODEL)),
                  _resident((2 * LANES, LANES))],
        out_specs=pl.BlockSpec((TM, D_MODEL), row),
        out_shape=jax.ShapeDtypeStruct((n_tiles * TM, D_MODEL), F32),
        compiler_params=_cparams(1),
        name="merge",
    )(h, mods, g, ya, yb, zr, zi, wg, wb, wo, cs)


def _dft_cos_sin(n, rows=None, cols=None):
    j = jnp.arange(n if rows is None else rows, dtype=jnp.int32)[:, None]
    k = jnp.arange(n if cols is None else cols, dtype=jnp.int32)[None, :]
    ang = ((j * k) % n).astype(F32) * (2.0 * math.pi / n)
    return jnp.cos(ang), jnp.sin(ang)


def _rope_tables():
    rows = SEQ // GRID_W
    row = jnp.repeat(jnp.arange(rows, dtype=F32), GRID_W)
    col = jnp.tile(jnp.arange(GRID_W, dtype=F32), rows)
    n_freq = HEAD_DIM // 4
    inv = jnp.power(ROPE_THETA, -jnp.arange(n_freq, dtype=F32) / n_freq)
    ang = jnp.concatenate([row[:, None] * inv, col[:, None] * inv], axis=-1)
    cos, sin = jnp.cos(ang), jnp.sin(ang)
    rc = jnp.tile(cos, (1, 4))
    rs = jnp.tile(jnp.concatenate([-sin, sin], axis=-1), (1, 2))
    rc = jnp.concatenate([rc, jnp.ones((TM, LANES), F32)], axis=0)
    rs = jnp.concatenate([rs, jnp.zeros((TM, LANES), F32)], axis=0)
    return rc, rs


def _pad_cols(w, n_groups, width, padded):
    d = w.shape[0]
    w = w.reshape(d, n_groups, width)
    return jnp.pad(w, ((0, 0), (0, 0), (0, padded - width))).reshape(d, n_groups * padded)


def _proj_weight(w_in_l):
    o = 0
    parts = []
    for n_groups, width, padded in ((8, 64, 128), (2, 64, 128), (2, 64, 128), (1, 512, 512),
                                    (1, 512, 512), (4, 128, 256), (1, 512, 512)):
        n = n_groups * width
        parts.append(_pad_cols(w_in_l[:, o:o + n], n_groups, width, padded))
        o += n
    return jnp.concatenate(parts, axis=1).astype(BF16), o


def kernel(x, c, ctx, c_ctx, w_ada, b_ada, norm_g, ffn_wi, ffn_wo, w_in, qk_g, diff_lam, diff_subln_g, w_branch, w_out):
    rc, rs = _rope_tables()
    c128, s128 = _dft_cos_sin(FFT_N2)
    f1 = (jnp.concatenate([c128, -s128], axis=0) * FFT_N2 ** -0.5).astype(BF16)
    c64, s64 = _dft_cos_sin(FFT_N1)
    g2 = (jnp.concatenate([jnp.concatenate([c64, s64], axis=1),
                           jnp.concatenate([-s64, c64], axis=1)], axis=0) * FFT_N1 ** -0.5).astype(BF16)
    twc, tws = _dft_cos_sin(SEQ, rows=FFT_N2, cols=FFT_N1)
    twc = jnp.repeat(twc, LANES, axis=1)
    tws = jnp.repeat(tws, LANES, axis=1)
    cctx, sctx = _dft_cos_sin(CTX_LEN)
    fctx = (jnp.concatenate([cctx, -sctx], axis=0) * CTX_LEN ** -0.5).astype(BF16)
    cs = (jnp.concatenate([c128, s128], axis=0) * LANES ** -0.5).astype(BF16)

    ct = jnp.zeros((D_MODEL, 8), F32).at[:, 0:BATCH].set(c.T).at[:, BATCH].set(c_ctx)
    mods_all = _ada_call(ct, w_ada, b_ada).reshape(DEPTH, 8, N_MOD, D_MODEL)[:, :3]

    h = jnp.concatenate([x.reshape(N_LAT, D_MODEL), ctx.reshape(N_CTX, D_MODEL)], axis=0)
    q_scale = jnp.array([HEAD_DIM ** -0.5, 1.0, HEAD_DIM ** -0.5, 1.0], F32)[:, None]
    for l in range(DEPTH):
        last = l == DEPTH - 1
        mods = mods_all[l]
        ng = norm_g[l].reshape(3, 1, D_MODEL)
        wi = []
        wo = []
        for k in range(2):
            w = ffn_wi[l, k]
            gate = w[:, :D_FF].reshape(D_MODEL, N_FF, FF_CHUNK)
            up = w[:, D_FF:].reshape(D_MODEL, N_FF, FF_CHUNK)
            wi.append(jnp.concatenate([gate, up], axis=-1).transpose(1, 0, 2).astype(BF16))
            wo.append(ffn_wo[l, k].reshape(N_FF, FF_CHUNK, D_MODEL).astype(BF16))
        w_proj, gate_off = _proj_weight(w_in[l])
        w_gate = w_in[l][:, gate_off:].reshape(D_MODEL, 3, D_MODEL).transpose(1, 0, 2).astype(BF16)
        gt = jnp.tile(qk_g[l], (1, 2)) * q_scale
        lam_init = 0.8 - 0.6 * math.exp(-0.3 * l)

        h = _ffn_call(h, mods, ng[0], wi[0], wo[0], k=0, n_tiles=NT_ALL)
        qa, ka, va, qb, kb, vb, uc = _proj_call(h, mods, ng[1], w_proj, gt, rc, rs)
        ya = _attn_a_call(qa, ka, va, with_ctx=not last)
        yb = _attn_b_call(qb, kb, vb, diff_lam[l], jnp.tile(diff_subln_g[l][None, :], (1, 1)),
                          with_ctx=not last, lam_init=lam_init)
        yr, yi = _fft1_call(uc.reshape(R_ALL * 512 // (FFT_N1 * 512), FFT_N1 * 512), f1, twc, tws)
        zr, zi = _fft2_call(yr, yi, g2)
        zr = zr.reshape(R_ALL, 512)
        zi = zi.reshape(R_ALL, 512)
        if not last:
            zr, zi = _fft_ctx_call(uc, fctx, zr, zi)
        n_tiles = NT_LAT if last else NT_ALL
        h = _merge_call(h, mods, ng[1], ya, yb, zr, zi, w_gate, w_branch[l].astype(BF16),
                        w_out[l].astype(BF16), cs, n_tiles=n_tiles)
        h = _ffn_call(h, mods, ng[2], wi[1], wo[1], k=2, n_tiles=n_tiles)
    return h.reshape(BATCH, SEQ, D_MODEL)
```

```python
import functools
import math

import jax
import jax.numpy as jnp
from jax import lax
from jax.experimental import pallas as pl
from jax.experimental.pallas import tpu as pltpu

F32 = jnp.float32
BF16 = jnp.bfloat16

D_MODEL = 1024
BATCH = 2
SEQ = 8192
DEPTH = 2
GRID_W = 64
CTX_LEN = 256
HEAD_DIM = 64
D_FF = 2816
N_MOD = 9
ROPE_THETA = 10000.0
EPS = 1e-6

LANES = 128
N_LAT = BATCH * SEQ
N_CTX = BATCH * CTX_LEN
R_ALL = N_LAT + N_CTX
TM = 512
NT_LAT = N_LAT // TM
NT_ALL = R_ALL // TM
TILES_PER_BATCH = SEQ // TM
FF_CHUNK = 256
N_FF = D_FF // FF_CHUNK
TQ = 256
NQ = SEQ // TQ
TK = 1024
ADA_TN = 1152
FFT_N1 = 64
FFT_N2 = 128
FFT_TN = 4096
FFT_KB = 16
VMEM_LIMIT = 56 * 1024 * 1024


def _cparams(n_axes):
    return pltpu.CompilerParams(dimension_semantics=("arbitrary",) * n_axes,
                                vmem_limit_bytes=VMEM_LIMIT)


def _resident(shape):
    nd = len(shape)
    return pl.BlockSpec(shape, lambda *_: (0,) * nd, pipeline_mode=pl.Buffered(1))


def _mod_set(i):
    return jnp.minimum(i // TILES_PER_BATCH, 2)


def _rms_mod(h, g, shift, scale):
    y = h * lax.rsqrt(jnp.mean(h * h, axis=-1, keepdims=True) + EPS) * g
    return y * (1.0 + scale) + shift


def _ada_kernel(ct_ref, w_ref, b_ref, o_ref):
    ct = ct_ref[...]
    s = ct * jax.nn.sigmoid(ct)
    w = w_ref[0]
    rows = [jnp.sum(w * s[:, r:r + 1], axis=0, keepdims=True) for r in range(3)]
    rows.append(jnp.zeros((5, ADA_TN), F32))
    o_ref[0] = jnp.concatenate(rows, axis=0) + b_ref[0]


def _ada_call(ct, w_ada, b_ada):
    n_col = N_MOD * D_MODEL
    return pl.pallas_call(
        _ada_kernel,
        grid=(DEPTH, n_col // ADA_TN),
        in_specs=[pl.BlockSpec((D_MODEL, 8), lambda l, j: (0, 0)),
                  pl.BlockSpec((1, D_MODEL, ADA_TN), lambda l, j: (l, 0, j)),
                  pl.BlockSpec((1, 1, ADA_TN), lambda l, j: (l, 0, j))],
        out_specs=pl.BlockSpec((1, 8, ADA_TN), lambda l, j: (l, 0, j)),
        out_shape=jax.ShapeDtypeStruct((DEPTH, 8, n_col), F32),
        compiler_params=_cparams(2),
        name="ada",
    )(ct, w_ada, b_ada.reshape(DEPTH, 1, n_col))


def _ffn_kernel(h_ref, mods_ref, g_ref, wi_ref, wo_ref, o_ref, acc_ref, *, k):
    h = h_ref[...]
    m = mods_ref[0]
    hn = _rms_mod(h, g_ref[...], m[3 * k:3 * k + 1], m[3 * k + 1:3 * k + 2]).astype(BF16)
    acc_ref[...] = jnp.zeros_like(acc_ref)

    def body(j, carry):
        gu = jnp.dot(hn, wi_ref[j], preferred_element_type=F32)
        g = gu[:, :FF_CHUNK]
        a = (g * jax.nn.sigmoid(g) * gu[:, FF_CHUNK:]).astype(BF16)
        acc_ref[...] += jnp.dot(a, wo_ref[j], preferred_element_type=F32)
        return carry

    lax.fori_loop(0, N_FF, body, 0)
    o_ref[...] = h + 0.5 * m[3 * k + 2:3 * k + 3] * acc_ref[...]


def _ffn_call(h, mods, g, wi, wo, *, k, n_tiles):
    rows = n_tiles * TM
    return pl.pallas_call(
        functools.partial(_ffn_kernel, k=k),
        grid=(n_tiles,),
        in_specs=[pl.BlockSpec((TM, D_MODEL), lambda i: (i, 0)),
                  pl.BlockSpec((1, N_MOD, D_MODEL), lambda i: (_mod_set(i), 0, 0)),
                  _resident((1, D_MODEL)),
                  _resident((N_FF, D_MODEL, 2 * FF_CHUNK)),
                  _resident((N_FF, FF_CHUNK, D_MODEL))],
        out_specs=pl.BlockSpec((TM, D_MODEL), lambda i: (i, 0)),
        out_shape=jax.ShapeDtypeStruct((rows, D_MODEL), F32),
        scratch_shapes=[pltpu.VMEM((TM, D_MODEL), F32)],
        compiler_params=_cparams(1),
        name=f"ffn{k}",
    )(h, mods, g, wi, wo)


N_PROJ = 4096


def _proj_kernel(h_ref, mods_ref, g_ref, w_ref, gt_ref, rc_ref, rs_ref,
                 qa_ref, ka_ref, va_ref, qb_ref, kb_ref, vb_ref, uc_ref):
    h = h_ref[...]
    m = mods_ref[0]
    hn = _rms_mod(h, g_ref[...], m[3:4], m[4:5]).astype(BF16)
    rc = rc_ref[...]
    rs = rs_ref[...]
    gt = gt_ref[...]
    lane = lax.broadcasted_iota(jnp.int32, (TM, LANES), 1)
    lo = lane < HEAD_DIM
    first_half = (lane % HEAD_DIM) < HEAD_DIM // 2
    inv_hd = 1.0 / HEAD_DIM

    def rope(y):
        swapped = jnp.where(first_half, pltpu.roll(y, 96, 1), pltpu.roll(y, 32, 1))
        return y * rc + swapped * rs

    def norm_one(z, g):
        ss = jnp.sum(z * z, axis=-1, keepdims=True) * inv_hd
        return rope(z * lax.rsqrt(ss + EPS) * g)

    def norm_two(z, g):
        zz = z * z
        s_lo = jnp.sum(jnp.where(lo, zz, 0.0), axis=-1, keepdims=True) * inv_hd
        s_hi = jnp.sum(jnp.where(lo, 0.0, zz), axis=-1, keepdims=True) * inv_hd
        r = jnp.where(lo, lax.rsqrt(s_lo + EPS), lax.rsqrt(s_hi + EPS))
        return rope(z * r * g)

    def chunk(c):
        return jnp.dot(hn, w_ref[:, 512 * c:512 * (c + 1)], preferred_element_type=F32)

    def slabs(z):
        return [z[:, LANES * j:LANES * (j + 1)] for j in range(4)]

    for c in range(2):
        z = chunk(c)
        qa_ref[:, 512 * c:512 * (c + 1)] = jnp.concatenate(
            [norm_one(s, gt[0:1]) for s in slabs(z)], axis=1).astype(BF16)
    s = slabs(chunk(2))
    ka_ref[...] = jnp.concatenate([norm_one(s[0], gt[1:2]), norm_one(s[1], gt[1:2])], axis=1).astype(BF16)
    ones_hi = jnp.where(lo, 0.0, 1.0)
    va_ref[...] = jnp.concatenate([s[2] + ones_hi, s[3] + ones_hi], axis=1).astype(BF16)
    qb_ref[...] = jnp.concatenate([norm_two(s, gt[2:3]) for s in slabs(chunk(3))], axis=1).astype(BF16)
    kb_ref[...] = jnp.concatenate([norm_two(s, gt[3:4]) for s in slabs(chunk(4))], axis=1).astype(BF16)
    for c in range(2):
        s = slabs(chunk(5 + c))
        vb_ref[:, 512 * c:512 * (c + 1)] = jnp.concatenate(
            [s[0], s[1] + 1.0, s[2], s[3] + 1.0], axis=1).astype(BF16)
    uc_ref[...] = chunk(7).astype(BF16)


def _proj_call(h, mods, g, w, gt, rc, rs):
    widths = (1024, 256, 256, 512, 512, 1024, 512)
    return pl.pallas_call(
        _proj_kernel,
        grid=(NT_ALL,),
        in_specs=[pl.BlockSpec((TM, D_MODEL), lambda i: (i, 0)),
                  pl.BlockSpec((1, N_MOD, D_MODEL), lambda i: (_mod_set(i), 0, 0)),
                  _resident((1, D_MODEL)),
                  _resident((D_MODEL, N_PROJ)),
                  _resident((4, LANES)),
                  pl.BlockSpec((TM, LANES), lambda i: (jnp.where(i < NT_LAT, i % TILES_PER_BATCH, TILES_PER_BATCH), 0)),
                  pl.BlockSpec((TM, LANES), lambda i: (jnp.where(i < NT_LAT, i % TILES_PER_BATCH, TILES_PER_BATCH), 0))],
        out_specs=[pl.BlockSpec((TM, wd), lambda i: (i, 0)) for wd in widths],
        out_shape=[jax.ShapeDtypeStruct((R_ALL, wd), BF16) for wd in widths],
        compiler_params=_cparams(1),
        name="proj",
    )(h, mods, g, w, gt, rc, rs)


N_LAT_CHUNKS = SEQ // TK
CTX_CHUNK = -1


def _attn_pipeline(chunks, qs, k_of, v_of, s_ref, p_ref, al_ref, m_ref, acc_ref):
    width = lambda c: CTX_LEN if c == CTX_CHUNK else TK

    def qk(i):
        for st, q in enumerate(qs):
            s_ref[st, i % 2, :, :width(chunks[i])] = lax.dot_general(
                q, k_of(chunks[i], st), (((1,), (1,)), ((), ())), preferred_element_type=F32)

    def softmax(i):
        for st in range(len(qs)):
            s = s_ref[st, i % 2, :, :width(chunks[i])]
            m_old = m_ref[st]
            m_new = jnp.maximum(m_old, jnp.max(s, axis=-1, keepdims=True))
            al_ref[st, i % 2] = jnp.exp2(m_old - m_new)
            for j in range(width(chunks[i]) // LANES):
                cols = slice(LANES * j, LANES * (j + 1))
                p_ref[st, i % 2, :, cols] = jnp.exp2(s[:, cols] - m_new).astype(BF16)
            m_ref[st] = m_new

    def pv(i):
        for st in range(len(qs)):
            o = jnp.dot(p_ref[st, i % 2, :, :width(chunks[i])], v_of(chunks[i], st), preferred_element_type=F32)
            al = al_ref[st, i % 2]
            for j in range(acc_ref.shape[2] // LANES):
                cols = slice(LANES * j, LANES * (j + 1))
                acc_ref[st, :, cols] = al * acc_ref[st, :, cols] + o[:, cols]

    m_ref[...] = jnp.full_like(m_ref, -1e30)
    acc_ref[...] = jnp.zeros_like(acc_ref)
    n = len(chunks)
    qk(0)
    if n > 1:
        qk(1)
    softmax(0)
    for i in range(n - 2):
        pv(i)
        qk(i + 2)
        softmax(i + 1)
    if n > 1:
        pv(n - 2)
        softmax(n - 1)
    pv(n - 1)


def _run_chunks(with_ctx, run):
    all_chunks = list(range(N_LAT_CHUNKS)) + [CTX_CHUNK]
    if not with_ctx:
        run(all_chunks)
        return
    qi = pl.program_id(2)

    @pl.when(qi < NQ)
    def _():
        run(all_chunks)

    @pl.when(qi == NQ)
    def _():
        run([CTX_CHUNK])


def _chunk_rows(c):
    return slice(None) if c == CTX_CHUNK else pl.ds(c * TK, TK)


def _attn_a_kernel(q_ref, kc_ref, vc_ref, kl_ref, vl_ref, o_ref, s_ref, p_ref, al_ref, m_ref, acc_ref, *, with_ctx):
    q4 = jnp.concatenate([q_ref[:, LANES * j:LANES * (j + 1)] for j in range(4)], axis=0)
    k_of = lambda c, st: (kc_ref if c == CTX_CHUNK else kl_ref)[_chunk_rows(c), :]
    v_of = lambda c, st: (vc_ref if c == CTX_CHUNK else vl_ref)[_chunk_rows(c), :]
    _run_chunks(with_ctx, lambda chunks: _attn_pipeline(chunks, [q4], k_of, v_of, s_ref, p_ref, al_ref, m_ref, acc_ref))
    acc = acc_ref[0]
    o = acc / pltpu.roll(acc, HEAD_DIM, 1)
    lo = lax.broadcasted_iota(jnp.int32, (TQ, LANES), 1) < HEAD_DIM
    pairs = [jnp.where(lo, o[2 * j * TQ:(2 * j + 1) * TQ], pltpu.roll(o[(2 * j + 1) * TQ:(2 * j + 2) * TQ], HEAD_DIM, 1))
             for j in range(2)]
    o_ref[...] = jnp.concatenate(pairs, axis=1).astype(BF16)


def _q_row_block(b, qi):
    return jnp.where(qi < NQ, b * NQ + qi, 2 * NQ + b)


def _attn_scratch(n_streams, rows, n_v):
    return [pltpu.VMEM((n_streams, 2, rows, TK), F32),
            pltpu.VMEM((n_streams, 2, rows, TK), BF16),
            pltpu.VMEM((n_streams, 2, rows, LANES), F32),
            pltpu.VMEM((n_streams, rows, LANES), F32),
            pltpu.VMEM((n_streams, rows, n_v), F32)]


def _attn_a_call(qa, ka, va, *, with_ctx):
    n_qt = NQ + (1 if with_ctx else 0)
    return pl.pallas_call(
        functools.partial(_attn_a_kernel, with_ctx=with_ctx),
        grid=(BATCH, 2, n_qt),
        in_specs=[pl.BlockSpec((TQ, 512), lambda b, g, qi: (_q_row_block(b, qi), g)),
                  pl.BlockSpec((CTX_LEN, LANES), lambda b, g, qi: (2 * NQ + b, g)),
                  pl.BlockSpec((CTX_LEN, LANES), lambda b, g, qi: (2 * NQ + b, g)),
                  pl.BlockSpec((SEQ, LANES), lambda b, g, qi: (b, g), pipeline_mode=pl.Buffered(1)),
                  pl.BlockSpec((SEQ, LANES), lambda b, g, qi: (b, g), pipeline_mode=pl.Buffered(1))],
        out_specs=pl.BlockSpec((TQ, 256), lambda b, g, qi: (_q_row_block(b, qi), g)),
        out_shape=jax.ShapeDtypeStruct((R_ALL if with_ctx else N_LAT, 512), BF16),
        scratch_shapes=_attn_scratch(1, 4 * TQ, LANES),
        compiler_params=_cparams(3),
        name="attn_a",
    )(qa, ka, va, ka, va)


def _attn_b_kernel(q_ref, kc_ref, vc_ref, kl_ref, vl_ref, dl_ref, sg_ref, o_ref,
                   s_ref, p_ref, al_ref, m_ref, acc_ref, *, with_ctx, lam_init):
    lo = lax.broadcasted_iota(jnp.int32, (TQ, LANES), 1) < HEAD_DIM
    zero = jnp.zeros((TQ, LANES), BF16)
    q2 = []
    for hh in range(2):
        qh = q_ref[:, LANES * hh:LANES * (hh + 1)]
        q2.append(jnp.concatenate([jnp.where(lo, qh, zero), jnp.where(lo, zero, qh)], axis=0))
    k_of = lambda c, hh: (kc_ref if c == CTX_CHUNK else kl_ref)[_chunk_rows(c), LANES * hh:LANES * (hh + 1)]
    v_of = lambda c, hh: (vc_ref if c == CTX_CHUNK else vl_ref)[_chunk_rows(c), 256 * hh:256 * (hh + 1)]
    _run_chunks(with_ctx, lambda chunks: _attn_pipeline(chunks, q2, k_of, v_of, s_ref, p_ref, al_ref, m_ref, acc_ref))

    lp = dl_ref[...]
    lam = (jnp.exp(jnp.sum(lp[0:1] * lp[1:2], axis=-1, keepdims=True))
           - jnp.exp(jnp.sum(lp[2:3] * lp[3:4], axis=-1, keepdims=True)) + lam_init)
    outs = []
    for hh in range(2):
        acc = acc_ref[hh]
        o = acc[:, :LANES] / acc[:, LANES:]
        d = o[:TQ] - lam * o[TQ:]
        y = d * lax.rsqrt(jnp.mean(d * d, axis=-1, keepdims=True) + EPS) * sg_ref[...]
        outs.append(y * (1.0 - lam_init))
    o_ref[...] = jnp.concatenate(outs, axis=1).astype(BF16)


def _attn_b_call(qb, kb, vb, diff_lam_l, subln_g, *, with_ctx, lam_init):
    n_qt = NQ + (1 if with_ctx else 0)
    return pl.pallas_call(
        functools.partial(_attn_b_kernel, with_ctx=with_ctx, lam_init=lam_init),
        grid=(BATCH, 2, n_qt),
        in_specs=[pl.BlockSpec((TQ, 256), lambda b, hp, qi: (_q_row_block(b, qi), hp)),
                  pl.BlockSpec((CTX_LEN, 256), lambda b, hp, qi: (2 * NQ + b, hp)),
                  pl.BlockSpec((CTX_LEN, 512), lambda b, hp, qi: (2 * NQ + b, hp)),
                  pl.BlockSpec((SEQ, 256), lambda b, hp, qi: (b, hp), pipeline_mode=pl.Buffered(1)),
                  pl.BlockSpec((SEQ, 512), lambda b, hp, qi: (b, hp), pipeline_mode=pl.Buffered(1)),
                  pl.BlockSpec((4, HEAD_DIM), lambda b, hp, qi: (0, 0)),
                  pl.BlockSpec((1, LANES), lambda b, hp, qi: (0, 0))],
        out_specs=pl.BlockSpec((TQ, 256), lambda b, hp, qi: (_q_row_block(b, qi), hp)),
        out_shape=jax.ShapeDtypeStruct((R_ALL if with_ctx else N_LAT, 512), BF16),
        scratch_shapes=_attn_scratch(2, 2 * TQ, 256),
        compiler_params=_cparams(3),
        name="attn_b",
    )(qb, kb, vb, kb, vb, diff_lam_l, subln_g)


def _fft1_kernel(x_ref, f_ref, tc_ref, ts_ref, yr_ref, yi_ref):
    y = jnp.dot(f_ref[...], x_ref[...], preferred_element_type=F32)
    for t in range(FFT_TN // 512):
        c = jnp.concatenate([tc_ref[:, LANES * t:LANES * (t + 1)]] * 4, axis=1)
        s = jnp.concatenate([ts_ref[:, LANES * t:LANES * (t + 1)]] * 4, axis=1)
        a = y[:FFT_N2, 512 * t:512 * (t + 1)]
        b = y[FFT_N2:, 512 * t:512 * (t + 1)]
        yr_ref[:, 512 * t:512 * (t + 1)] = (a * c + b * s).astype(BF16)
        yi_ref[:, 512 * t:512 * (t + 1)] = (b * c - a * s).astype(BF16)


def _fft1_call(uc_view, f_stack, tc, ts):
    n_col = FFT_N1 * 512
    return pl.pallas_call(
        _fft1_kernel,
        grid=(BATCH, n_col // FFT_TN),
        in_specs=[pl.BlockSpec((FFT_N2, FFT_TN), lambda b, j: (b, j)),
                  pl.BlockSpec((2 * FFT_N2, FFT_N2), lambda b, j: (0, 0)),
                  pl.BlockSpec((FFT_N2, FFT_TN // 4), lambda b, j: (0, j)),
                  pl.BlockSpec((FFT_N2, FFT_TN // 4), lambda b, j: (0, j))],
        out_specs=[pl.BlockSpec((FFT_N2, FFT_TN), lambda b, j: (b, j))] * 2,
        out_shape=[jax.ShapeDtypeStruct((BATCH * FFT_N2, n_col), BF16)] * 2,
        compiler_params=_cparams(2),
        name="fft1",
    )(uc_view, f_stack, tc, ts)


def _fft2_kernel(yr_ref, yi_ref, g_ref, zr_ref, zi_ref):
    for t in range(FFT_KB):
        ys = jnp.concatenate([yr_ref[t], yi_ref[t]], axis=0)
        z = jnp.dot(g_ref[...], ys, preferred_element_type=F32)
        zr_ref[:, 512 * t:512 * (t + 1)] = z[:FFT_N1].astype(BF16)
        zi_ref[:, 512 * t:512 * (t + 1)] = z[FFT_N1:].astype(BF16)


def _fft2_call(yr, yi, g_stack):
    steps = FFT_N2 // FFT_KB
    return pl.pallas_call(
        _fft2_kernel,
        grid=(BATCH, steps),
        in_specs=[pl.BlockSpec((FFT_KB, FFT_N1, 512), lambda b, j: (b * steps + j, 0, 0)),
                  pl.BlockSpec((FFT_KB, FFT_N1, 512), lambda b, j: (b * steps + j, 0, 0)),
                  pl.BlockSpec((2 * FFT_N1, 2 * FFT_N1), lambda b, j: (0, 0))],
        out_specs=[pl.BlockSpec((FFT_N1, FFT_KB * 512), lambda b, j: (b, j))] * 2,
        out_shape=[jax.ShapeDtypeStruct((BATCH * FFT_N1, FFT_N2 * 512), BF16)] * 2,
        compiler_params=_cparams(2),
        name="fft2",
    )(yr.reshape(BATCH * FFT_N2, FFT_N1, 512), yi.reshape(BATCH * FFT_N2, FFT_N1, 512), g_stack)


def _fft_ctx_kernel(x_ref, f_ref, zr_ref, zi_ref):
    z = jnp.dot(f_ref[...], x_ref[...], preferred_element_type=F32)
    zr_ref[...] = z[:CTX_LEN].astype(BF16)
    zi_ref[...] = z[CTX_LEN:].astype(BF16)


def _fft_ctx_call(uc, f_stack):
    return pl.pallas_call(
        _fft_ctx_kernel,
        grid=(BATCH,),
        in_specs=[pl.BlockSpec((CTX_LEN, 512), lambda b: (2 * NQ + b, 0)),
                  pl.BlockSpec((2 * CTX_LEN, CTX_LEN), lambda b: (0, 0))],
        out_specs=[pl.BlockSpec((CTX_LEN, 512), lambda b: (b, 0))] * 2,
        out_shape=[jax.ShapeDtypeStruct((N_CTX, 512), BF16)] * 2,
        compiler_params=_cparams(1),
        name="fft_ctx",
    )(uc, f_stack)


def _merge_kernel(h_ref, mods_ref, g_ref, ya_ref, yb_ref, zr_ref, zi_ref, *rest, with_ctx):
    if with_ctx:
        zrc_ref, zic_ref, wg_ref, wb_ref, wo_ref, cs_ref, o_ref = rest
        is_ctx = pl.program_id(0) == NT_LAT
        zr = jnp.where(is_ctx, zrc_ref[...], zr_ref[...])
        zi = jnp.where(is_ctx, zic_ref[...], zi_ref[...])
    else:
        wg_ref, wb_ref, wo_ref, cs_ref, o_ref = rest
        zr = zr_ref[...]
        zi = zi_ref[...]
    h = h_ref[...]
    m = mods_ref[0]
    hn = _rms_mod(h, g_ref[...], m[3:4], m[4:5]).astype(BF16)
    yc = jnp.concatenate(
        [jnp.dot(jnp.concatenate([zr[:, LANES * g:LANES * (g + 1)], zi[:, LANES * g:LANES * (g + 1)]], axis=1),
                 cs_ref[...], preferred_element_type=F32) for g in range(4)], axis=1).astype(BF16)
    branches = (ya_ref[...], yb_ref[...], yc)
    mix = jnp.zeros((TM, D_MODEL), F32)
    for i in range(3):
        gl = jnp.dot(hn, wg_ref[i], preferred_element_type=F32)
        pr = jnp.dot(branches[i], wb_ref[i], preferred_element_type=F32)
        mix = mix + jax.nn.sigmoid(gl) * pr
    o_ref[...] = h + m[5:6] * jnp.dot(mix.astype(BF16), wo_ref[...], preferred_element_type=F32)


def _merge_call(h, mods, g, ya, yb, zr, zi, z_ctx, wg, wb, wo, cs):
    with_ctx = z_ctx is not None
    n_tiles = NT_ALL if with_ctx else NT_LAT
    row = lambda i: (i, 0)
    lat_row = lambda i: (jnp.minimum(i, NT_LAT - 1), 0)
    ctx_specs = [_resident((N_CTX, 512))] * 2 if with_ctx else []
    return pl.pallas_call(
        functools.partial(_merge_kernel, with_ctx=with_ctx),
        grid=(n_tiles,),
        in_specs=[pl.BlockSpec((TM, D_MODEL), row),
                  pl.BlockSpec((1, N_MOD, D_MODEL), lambda i: (_mod_set(i), 0, 0)),
                  _resident((1, D_MODEL)),
                  pl.BlockSpec((TM, 512), row), pl.BlockSpec((TM, 512), row),
                  pl.BlockSpec((TM, 512), lat_row), pl.BlockSpec((TM, 512), lat_row)]
                 + ctx_specs
                 + [_resident((3, D_MODEL, D_MODEL)),
                    _resident((3, 512, D_MODEL)),
                    _resident((D_MODEL, D_MODEL)),
                    _resident((2 * LANES, LANES))],
        out_specs=pl.BlockSpec((TM, D_MODEL), row),
        out_shape=jax.ShapeDtypeStruct((n_tiles * TM, D_MODEL), F32),
        compiler_params=_cparams(1),
        name="merge",
    )(h, mods, g, ya, yb, zr, zi, *(z_ctx or ()), wg, wb, wo, cs)


def _dft_cos_sin(n, rows=None, cols=None):
    j = jnp.arange(n if rows is None else rows, dtype=jnp.int32)[:, None]
    k = jnp.arange(n if cols is None else cols, dtype=jnp.int32)[None, :]
    ang = ((j * k) % n).astype(F32) * (2.0 * math.pi / n)
    return jnp.cos(ang), jnp.sin(ang)


def _rope_tables():
    rows = SEQ // GRID_W
    row = jnp.repeat(jnp.arange(rows, dtype=F32), GRID_W)
    col = jnp.tile(jnp.arange(GRID_W, dtype=F32), rows)
    n_freq = HEAD_DIM // 4
    inv = jnp.power(ROPE_THETA, -jnp.arange(n_freq, dtype=F32) / n_freq)
    ang = jnp.concatenate([row[:, None] * inv, col[:, None] * inv], axis=-1)
    cos, sin = jnp.cos(ang), jnp.sin(ang)
    rc = jnp.tile(cos, (1, 4))
    rs = jnp.tile(jnp.concatenate([-sin, sin], axis=-1), (1, 2))
    rc = jnp.concatenate([rc, jnp.ones((TM, LANES), F32)], axis=0)
    rs = jnp.concatenate([rs, jnp.zeros((TM, LANES), F32)], axis=0)
    return rc, rs


def _pad_cols(w, n_groups, width, padded):
    d = w.shape[0]
    w = w.reshape(d, n_groups, width)
    return jnp.pad(w, ((0, 0), (0, 0), (0, padded - width))).reshape(d, n_groups * padded)


def _proj_weight(w_in_l):
    o = 0
    parts = []
    for n_groups, width, padded in ((8, 64, 128), (2, 64, 128), (2, 64, 128), (1, 512, 512),
                                    (1, 512, 512), (4, 128, 256), (1, 512, 512)):
        n = n_groups * width
        parts.append(_pad_cols(w_in_l[:, o:o + n], n_groups, width, padded))
        o += n
    return jnp.concatenate(parts, axis=1).astype(BF16), o


def kernel(x, c, ctx, c_ctx, w_ada, b_ada, norm_g, ffn_wi, ffn_wo, w_in, qk_g, diff_lam, diff_subln_g, w_branch, w_out):
    rc, rs = _rope_tables()
    c128, s128 = _dft_cos_sin(FFT_N2)
    f1 = (jnp.concatenate([c128, -s128], axis=0) * FFT_N2 ** -0.5).astype(BF16)
    c64, s64 = _dft_cos_sin(FFT_N1)
    g2 = (jnp.concatenate([jnp.concatenate([c64, s64], axis=1),
                           jnp.concatenate([-s64, c64], axis=1)], axis=0) * FFT_N1 ** -0.5).astype(BF16)
    twc, tws = _dft_cos_sin(SEQ, rows=FFT_N2, cols=FFT_N1)
    twc = jnp.repeat(twc, LANES, axis=1)
    tws = jnp.repeat(tws, LANES, axis=1)
    cctx, sctx = _dft_cos_sin(CTX_LEN)
    fctx = (jnp.concatenate([cctx, -sctx], axis=0) * CTX_LEN ** -0.5).astype(BF16)
    cs = (jnp.concatenate([c128, s128], axis=0) * LANES ** -0.5).astype(BF16)

    ct = jnp.zeros((D_MODEL, 8), F32).at[:, 0:BATCH].set(c.T).at[:, BATCH].set(c_ctx)
    mods_all = _ada_call(ct, w_ada, b_ada).reshape(DEPTH, 8, N_MOD, D_MODEL)[:, :3]

    h = jnp.concatenate([x.reshape(N_LAT, D_MODEL), ctx.reshape(N_CTX, D_MODEL)], axis=0)
    q_mul = HEAD_DIM ** -0.5 * math.log2(math.e)
    q_scale = jnp.array([q_mul, 1.0, q_mul, 1.0], F32)[:, None]
    for l in range(DEPTH):
        last = l == DEPTH - 1
        mods = mods_all[l]
        ng = norm_g[l].reshape(3, 1, D_MODEL)
        wi = []
        wo = []
        for k in range(2):
            w = ffn_wi[l, k]
            gate = w[:, :D_FF].reshape(D_MODEL, N_FF, FF_CHUNK)
            up = w[:, D_FF:].reshape(D_MODEL, N_FF, FF_CHUNK)
            wi.append(jnp.concatenate([gate, up], axis=-1).transpose(1, 0, 2).astype(BF16))
            wo.append(ffn_wo[l, k].reshape(N_FF, FF_CHUNK, D_MODEL).astype(BF16))
        w_proj, gate_off = _proj_weight(w_in[l])
        w_gate = w_in[l][:, gate_off:].reshape(D_MODEL, 3, D_MODEL).transpose(1, 0, 2).astype(BF16)
        gt = jnp.tile(qk_g[l], (1, 2)) * q_scale
        lam_init = 0.8 - 0.6 * math.exp(-0.3 * l)

        h = _ffn_call(h, mods, ng[0], wi[0], wo[0], k=0, n_tiles=NT_ALL)
        qa, ka, va, qb, kb, vb, uc = _proj_call(h, mods, ng[1], w_proj, gt, rc, rs)
        ya = _attn_a_call(qa, ka, va, with_ctx=not last)
        yb = _attn_b_call(qb, kb, vb, diff_lam[l], jnp.tile(diff_subln_g[l][None, :], (1, 1)),
                          with_ctx=not last, lam_init=lam_init)
        yr, yi = _fft1_call(uc.reshape(R_ALL * 512 // (FFT_N1 * 512), FFT_N1 * 512), f1, twc, tws)
        zr, zi = _fft2_call(yr, yi, g2)
        z_ctx = None if last else _fft_ctx_call(uc, fctx)
        h = _merge_call(h, mods, ng[1], ya, yb, zr.reshape(N_LAT, 512), zi.reshape(N_LAT, 512), z_ctx,
                        w_gate, w_branch[l].astype(BF16), w_out[l].astype(BF16), cs)
        n_tiles = NT_LAT if last else NT_ALL
        h = _ffn_call(h, mods, ng[2], wi[1], wo[1], k=2, n_tiles=n_tiles)
    return h.reshape(BATCH, SEQ, D_MODEL)
```

```python
import functools
import math

import jax
import jax.numpy as jnp
from jax import lax
from jax.experimental import pallas as pl
from jax.experimental.pallas import tpu as pltpu

F32 = jnp.float32
BF16 = jnp.bfloat16

D_MODEL = 1024
BATCH = 2
SEQ = 8192
DEPTH = 2
GRID_W = 64
CTX_LEN = 256
HEAD_DIM = 64
D_FF = 2816
N_MOD = 9
ROPE_THETA = 10000.0
EPS = 1e-6

LANES = 128
N_LAT = BATCH * SEQ
N_CTX = BATCH * CTX_LEN
R_ALL = N_LAT + N_CTX
TM = 512
NT_LAT = N_LAT // TM
NT_ALL = R_ALL // TM
TILES_PER_BATCH = SEQ // TM
FF_CHUNK = 256
N_FF = D_FF // FF_CHUNK
TQ = 256
NQ = SEQ // TQ
TK = 1024
ADA_TN = 1152
FFT_N1 = 64
FFT_N2 = 128
FFT_TN = 4096
FFT_KB = 16
VMEM_LIMIT = 56 * 1024 * 1024


def _cparams(n_axes):
    return pltpu.CompilerParams(dimension_semantics=("arbitrary",) * n_axes,
                                vmem_limit_bytes=VMEM_LIMIT)


def _resident(shape):
    nd = len(shape)
    return pl.BlockSpec(shape, lambda *_: (0,) * nd, pipeline_mode=pl.Buffered(1))


def _mod_set(i):
    return jnp.minimum(i // TILES_PER_BATCH, 2)


def _rms_mod(h, g, shift, scale):
    y = h * lax.rsqrt(jnp.mean(h * h, axis=-1, keepdims=True) + EPS) * g
    return y * (1.0 + scale) + shift


def _ada_kernel(ct_ref, w_ref, b_ref, o_ref):
    ct = ct_ref[...]
    s = ct * jax.nn.sigmoid(ct)
    w = w_ref[0]
    rows = [jnp.sum(w * s[:, r:r + 1], axis=0, keepdims=True) for r in range(3)]
    rows.append(jnp.zeros((5, ADA_TN), F32))
    o_ref[0] = jnp.concatenate(rows, axis=0) + b_ref[0]


def _ada_call(ct, w_ada, b_ada):
    n_col = N_MOD * D_MODEL
    return pl.pallas_call(
        _ada_kernel,
        grid=(DEPTH, n_col // ADA_TN),
        in_specs=[pl.BlockSpec((D_MODEL, 8), lambda l, j: (0, 0)),
                  pl.BlockSpec((1, D_MODEL, ADA_TN), lambda l, j: (l, 0, j)),
                  pl.BlockSpec((1, 1, ADA_TN), lambda l, j: (l, 0, j))],
        out_specs=pl.BlockSpec((1, 8, ADA_TN), lambda l, j: (l, 0, j)),
        out_shape=jax.ShapeDtypeStruct((DEPTH, 8, n_col), F32),
        compiler_params=_cparams(2),
        name="ada",
    )(ct, w_ada, b_ada.reshape(DEPTH, 1, n_col))


def _ffn_kernel(h_ref, mods_ref, g_ref, wi_ref, wo_ref, o_ref, *, k):
    h = h_ref[...]
    m = mods_ref[0]
    hn = _rms_mod(h, g_ref[...], m[3 * k:3 * k + 1], m[3 * k + 1:3 * k + 2]).astype(BF16)
    acc = None
    for j in range(N_FF):
        cols = slice(FF_CHUNK * j, FF_CHUNK * (j + 1))
        g = jnp.dot(hn, wi_ref[:, cols], preferred_element_type=F32)
        u = jnp.dot(hn, wi_ref[:, D_FF + FF_CHUNK * j:D_FF + FF_CHUNK * (j + 1)], preferred_element_type=F32)
        a = (g * jax.nn.sigmoid(g) * u).astype(BF16)
        o = jnp.dot(a, wo_ref[cols, :], preferred_element_type=F32)
        acc = o if acc is None else acc + o
    o_ref[...] = h + 0.5 * m[3 * k + 2:3 * k + 3] * acc


def _ffn_call(h, mods, g, wi, wo, *, k, n_tiles):
    rows = n_tiles * TM
    return pl.pallas_call(
        functools.partial(_ffn_kernel, k=k),
        grid=(n_tiles,),
        in_specs=[pl.BlockSpec((TM, D_MODEL), lambda i: (i, 0)),
                  pl.BlockSpec((1, N_MOD, D_MODEL), lambda i: (_mod_set(i), 0, 0)),
                  _resident((1, D_MODEL)),
                  _resident((D_MODEL, 2 * D_FF)),
                  _resident((D_FF, D_MODEL))],
        out_specs=pl.BlockSpec((TM, D_MODEL), lambda i: (i, 0)),
        out_shape=jax.ShapeDtypeStruct((rows, D_MODEL), F32),
        compiler_params=_cparams(1),
        name=f"ffn{k}",
    )(h, mods, g, wi, wo)


N_PROJ = 4096


def _proj_kernel(h_ref, mods_ref, g_ref, w_ref, gt_ref, rc_ref, rs_ref,
                 qa_ref, ka_ref, va_ref, qb_ref, kb_ref, vb_ref, uc_ref):
    h = h_ref[...]
    m = mods_ref[0]
    hn = _rms_mod(h, g_ref[...], m[3:4], m[4:5]).astype(BF16)
    rc = rc_ref[...]
    rs = rs_ref[...]
    gt = gt_ref[...]
    lane = lax.broadcasted_iota(jnp.int32, (TM, LANES), 1)
    lo = lane < HEAD_DIM
    first_half = (lane % HEAD_DIM) < HEAD_DIM // 2
    inv_hd = 1.0 / HEAD_DIM

    def rope(y):
        swapped = jnp.where(first_half, pltpu.roll(y, 96, 1), pltpu.roll(y, 32, 1))
        return y * rc + swapped * rs

    def norm_one(z, g):
        ss = jnp.sum(z * z, axis=-1, keepdims=True) * inv_hd
        return rope(z * lax.rsqrt(ss + EPS) * g)

    def norm_two(z, g):
        zz = z * z
        s_lo = jnp.sum(jnp.where(lo, zz, 0.0), axis=-1, keepdims=True) * inv_hd
        s_hi = jnp.sum(jnp.where(lo, 0.0, zz), axis=-1, keepdims=True) * inv_hd
        r = jnp.where(lo, lax.rsqrt(s_lo + EPS), lax.rsqrt(s_hi + EPS))
        return rope(z * r * g)

    def chunk(c):
        return jnp.dot(hn, w_ref[:, 512 * c:512 * (c + 1)], preferred_element_type=F32)

    def slabs(z):
        return [z[:, LANES * j:LANES * (j + 1)] for j in range(4)]

    for c in range(2):
        z = chunk(c)
        qa_ref[:, 512 * c:512 * (c + 1)] = jnp.concatenate(
            [norm_one(s, gt[0:1]) for s in slabs(z)], axis=1).astype(BF16)
    s = slabs(chunk(2))
    ka_ref[...] = jnp.concatenate([norm_one(s[0], gt[1:2]), norm_one(s[1], gt[1:2])], axis=1).astype(BF16)
    ones_hi = jnp.where(lo, 0.0, 1.0)
    va_ref[...] = jnp.concatenate([s[2] + ones_hi, s[3] + ones_hi], axis=1).astype(BF16)
    qb_ref[...] = jnp.concatenate([norm_two(s, gt[2:3]) for s in slabs(chunk(3))], axis=1).astype(BF16)
    kb_ref[...] = jnp.concatenate([norm_two(s, gt[3:4]) for s in slabs(chunk(4))], axis=1).astype(BF16)
    for c in range(2):
        s = slabs(chunk(5 + c))
        vb_ref[:, 512 * c:512 * (c + 1)] = jnp.concatenate(
            [s[0], s[1] + 1.0, s[2], s[3] + 1.0], axis=1).astype(BF16)
    uc_ref[...] = chunk(7).astype(BF16)


def _proj_call(h, mods, g, w, gt, rc, rs):
    widths = (1024, 256, 256, 512, 512, 1024, 512)
    return pl.pallas_call(
        _proj_kernel,
        grid=(NT_ALL,),
        in_specs=[pl.BlockSpec((TM, D_MODEL), lambda i: (i, 0)),
                  pl.BlockSpec((1, N_MOD, D_MODEL), lambda i: (_mod_set(i), 0, 0)),
                  _resident((1, D_MODEL)),
                  _resident((D_MODEL, N_PROJ)),
                  _resident((4, LANES)),
                  pl.BlockSpec((TM, LANES), lambda i: (jnp.where(i < NT_LAT, i % TILES_PER_BATCH, TILES_PER_BATCH), 0)),
                  pl.BlockSpec((TM, LANES), lambda i: (jnp.where(i < NT_LAT, i % TILES_PER_BATCH, TILES_PER_BATCH), 0))],
        out_specs=[pl.BlockSpec((TM, wd), lambda i: (i, 0)) for wd in widths],
        out_shape=[jax.ShapeDtypeStruct((R_ALL, wd), BF16) for wd in widths],
        compiler_params=_cparams(1),
        name="proj",
    )(h, mods, g, w, gt, rc, rs)


N_LAT_CHUNKS = SEQ // TK
CTX_CHUNK = -1


def _attn_pipeline(chunks, qs, k_of, v_of, s_ref, p_ref, al_ref, m_ref, acc_ref):
    width = lambda c: CTX_LEN if c == CTX_CHUNK else TK

    def qk(i):
        for st, q in enumerate(qs):
            s_ref[st, i % 2, :, :width(chunks[i])] = lax.dot_general(
                q, k_of(chunks[i], st), (((1,), (1,)), ((), ())), preferred_element_type=F32)

    def softmax(i):
        for st in range(len(qs)):
            s = s_ref[st, i % 2, :, :width(chunks[i])]
            m_old = m_ref[st]
            m_new = jnp.maximum(m_old, jnp.max(s, axis=-1, keepdims=True))
            al_ref[st, i % 2] = jnp.exp2(m_old - m_new)
            for j in range(width(chunks[i]) // LANES):
                cols = slice(LANES * j, LANES * (j + 1))
                p_ref[st, i % 2, :, cols] = jnp.exp2(s[:, cols] - m_new).astype(BF16)
            m_ref[st] = m_new

    def pv(i):
        for st in range(len(qs)):
            o = jnp.dot(p_ref[st, i % 2, :, :width(chunks[i])], v_of(chunks[i], st), preferred_element_type=F32)
            al = al_ref[st, i % 2]
            for j in range(acc_ref.shape[2] // LANES):
                cols = slice(LANES * j, LANES * (j + 1))
                acc_ref[st, :, cols] = al * acc_ref[st, :, cols] + o[:, cols]

    m_ref[...] = jnp.full_like(m_ref, -1e30)
    acc_ref[...] = jnp.zeros_like(acc_ref)
    n = len(chunks)
    qk(0)
    if n > 1:
        qk(1)
    softmax(0)
    for i in range(n - 2):
        pv(i)
        qk(i + 2)
        softmax(i + 1)
    if n > 1:
        pv(n - 2)
        softmax(n - 1)
    pv(n - 1)


def _chunk_list(ctx_queries):
    return [CTX_CHUNK] if ctx_queries else list(range(N_LAT_CHUNKS)) + [CTX_CHUNK]


def _chunk_rows(c):
    return slice(None) if c == CTX_CHUNK else pl.ds(c * TK, TK)


def _split_attn_refs(refs, ctx_queries, n_extra):
    q_ref, kc_ref, vc_ref = refs[:3]
    kl_ref, vl_ref = (None, None) if ctx_queries else refs[3:5]
    rest = refs[3 if ctx_queries else 5:]
    return q_ref, kc_ref, vc_ref, kl_ref, vl_ref, rest[:n_extra], rest[n_extra], rest[n_extra + 1:]


def _attn_a_kernel(*refs, ctx_queries):
    q_ref, kc_ref, vc_ref, kl_ref, vl_ref, _, o_ref, scratch = _split_attn_refs(refs, ctx_queries, 0)
    q4 = jnp.concatenate([q_ref[:, LANES * j:LANES * (j + 1)] for j in range(4)], axis=0)
    k_of = lambda c, st: (kc_ref if c == CTX_CHUNK else kl_ref)[_chunk_rows(c), :]
    v_of = lambda c, st: (vc_ref if c == CTX_CHUNK else vl_ref)[_chunk_rows(c), :]
    _attn_pipeline(_chunk_list(ctx_queries), [q4], k_of, v_of, *scratch)
    acc = scratch[-1][0]
    o = acc / pltpu.roll(acc, HEAD_DIM, 1)
    lo = lax.broadcasted_iota(jnp.int32, (TQ, LANES), 1) < HEAD_DIM
    pairs = [jnp.where(lo, o[2 * j * TQ:(2 * j + 1) * TQ], pltpu.roll(o[(2 * j + 1) * TQ:(2 * j + 2) * TQ], HEAD_DIM, 1))
             for j in range(2)]
    o_ref[...] = jnp.concatenate(pairs, axis=1).astype(BF16)


def _attn_scratch(n_streams, rows, n_v):
    return [pltpu.VMEM((n_streams, 2, rows, TK), F32),
            pltpu.VMEM((n_streams, 2, rows, TK), BF16),
            pltpu.VMEM((n_streams, 2, rows, LANES), F32),
            pltpu.VMEM((n_streams, rows, LANES), F32),
            pltpu.VMEM((n_streams, rows, n_v), F32)]


def _attn_specs(ctx_queries, q_w, k_w, v_w):
    ctx_blk = lambda b, g, qi: (2 * NQ + b, g)
    q_blk = ctx_blk if ctx_queries else (lambda b, g, qi: (b * NQ + qi, g))
    o_blk = (lambda b, g, qi: (b, g)) if ctx_queries else q_blk
    specs = [pl.BlockSpec((TQ, q_w), q_blk),
             pl.BlockSpec((CTX_LEN, k_w), ctx_blk),
             pl.BlockSpec((CTX_LEN, v_w), ctx_blk)]
    if not ctx_queries:
        lat_blk = lambda b, g, qi: (b, g)
        specs += [pl.BlockSpec((SEQ, k_w), lat_blk, pipeline_mode=pl.Buffered(1)),
                  pl.BlockSpec((SEQ, v_w), lat_blk, pipeline_mode=pl.Buffered(1))]
    return specs, pl.BlockSpec((TQ, 256), o_blk)


def _attn_a_call(qa, ka, va, *, ctx_queries):
    in_specs, out_spec = _attn_specs(ctx_queries, 512, LANES, LANES)
    kv = (ka, va) if ctx_queries else (ka, va, ka, va)
    return pl.pallas_call(
        functools.partial(_attn_a_kernel, ctx_queries=ctx_queries),
        grid=(BATCH, 2, 1 if ctx_queries else NQ),
        in_specs=in_specs,
        out_specs=out_spec,
        out_shape=jax.ShapeDtypeStruct((N_CTX if ctx_queries else N_LAT, 512), BF16),
        scratch_shapes=_attn_scratch(1, 4 * TQ, LANES),
        compiler_params=_cparams(3),
        name="attn_a_ctx" if ctx_queries else "attn_a",
    )(qa, *kv)


def _attn_b_kernel(*refs, ctx_queries, lam_init):
    q_ref, kc_ref, vc_ref, kl_ref, vl_ref, (dl_ref, sg_ref), o_ref, scratch = _split_attn_refs(refs, ctx_queries, 2)
    lo = lax.broadcasted_iota(jnp.int32, (TQ, LANES), 1) < HEAD_DIM
    zero = jnp.zeros((TQ, LANES), BF16)
    q2 = []
    for hh in range(2):
        qh = q_ref[:, LANES * hh:LANES * (hh + 1)]
        q2.append(jnp.concatenate([jnp.where(lo, qh, zero), jnp.where(lo, zero, qh)], axis=0))
    k_of = lambda c, hh: (kc_ref if c == CTX_CHUNK else kl_ref)[_chunk_rows(c), LANES * hh:LANES * (hh + 1)]
    v_of = lambda c, hh: (vc_ref if c == CTX_CHUNK else vl_ref)[_chunk_rows(c), 256 * hh:256 * (hh + 1)]
    _attn_pipeline(_chunk_list(ctx_queries), q2, k_of, v_of, *scratch)

    lp = dl_ref[...]
    lam = (jnp.exp(jnp.sum(lp[0:1] * lp[1:2], axis=-1, keepdims=True))
           - jnp.exp(jnp.sum(lp[2:3] * lp[3:4], axis=-1, keepdims=True)) + lam_init)
    outs = []
    for hh in range(2):
        acc = scratch[-1][hh]
        o = acc[:, :LANES] / acc[:, LANES:]
        d = o[:TQ] - lam * o[TQ:]
        y = d * lax.rsqrt(jnp.mean(d * d, axis=-1, keepdims=True) + EPS) * sg_ref[...]
        outs.append(y * (1.0 - lam_init))
    o_ref[...] = jnp.concatenate(outs, axis=1).astype(BF16)


def _attn_b_call(qb, kb, vb, diff_lam_l, subln_g, *, ctx_queries, lam_init):
    in_specs, out_spec = _attn_specs(ctx_queries, 256, 256, 512)
    in_specs += [pl.BlockSpec((4, HEAD_DIM), lambda b, hp, qi: (0, 0)),
                 pl.BlockSpec((1, LANES), lambda b, hp, qi: (0, 0))]
    kv = (kb, vb) if ctx_queries else (kb, vb, kb, vb)
    return pl.pallas_call(
        functools.partial(_attn_b_kernel, ctx_queries=ctx_queries, lam_init=lam_init),
        grid=(BATCH, 2, 1 if ctx_queries else NQ),
        in_specs=in_specs,
        out_specs=out_spec,
        out_shape=jax.ShapeDtypeStruct((N_CTX if ctx_queries else N_LAT, 512), BF16),
        scratch_shapes=_attn_scratch(2, 2 * TQ, 256),
        compiler_params=_cparams(3),
        name="attn_b_ctx" if ctx_queries else "attn_b",
    )(qb, *kv, diff_lam_l, subln_g)


def _fft1_kernel(x_ref, f_ref, tc_ref, ts_ref, yr_ref, yi_ref):
    y = jnp.dot(f_ref[...], x_ref[...], preferred_element_type=F32)
    for t in range(FFT_TN // 512):
        c = jnp.concatenate([tc_ref[:, LANES * t:LANES * (t + 1)]] * 4, axis=1)
        s = jnp.concatenate([ts_ref[:, LANES * t:LANES * (t + 1)]] * 4, axis=1)
        a = y[:FFT_N2, 512 * t:512 * (t + 1)]
        b = y[FFT_N2:, 512 * t:512 * (t + 1)]
        yr_ref[:, 512 * t:512 * (t + 1)] = (a * c + b * s).astype(BF16)
        yi_ref[:, 512 * t:512 * (t + 1)] = (b * c - a * s).astype(BF16)


def _fft1_call(uc_view, f_stack, tc, ts):
    n_col = FFT_N1 * 512
    return pl.pallas_call(
        _fft1_kernel,
        grid=(BATCH, n_col // FFT_TN),
        in_specs=[pl.BlockSpec((FFT_N2, FFT_TN), lambda b, j: (b, j)),
                  pl.BlockSpec((2 * FFT_N2, FFT_N2), lambda b, j: (0, 0)),
                  pl.BlockSpec((FFT_N2, FFT_TN // 4), lambda b, j: (0, j)),
                  pl.BlockSpec((FFT_N2, FFT_TN // 4), lambda b, j: (0, j))],
        out_specs=[pl.BlockSpec((FFT_N2, FFT_TN), lambda b, j: (b, j))] * 2,
        out_shape=[jax.ShapeDtypeStruct((BATCH * FFT_N2, n_col), BF16)] * 2,
        compiler_params=_cparams(2),
        name="fft1",
    )(uc_view, f_stack, tc, ts)


def _fft2_kernel(yr_ref, yi_ref, g_ref, zr_ref, zi_ref):
    for t in range(FFT_KB):
        ys = jnp.concatenate([yr_ref[t], yi_ref[t]], axis=0)
        z = jnp.dot(g_ref[...], ys, preferred_element_type=F32)
        zr_ref[:, 512 * t:512 * (t + 1)] = z[:FFT_N1].astype(BF16)
        zi_ref[:, 512 * t:512 * (t + 1)] = z[FFT_N1:].astype(BF16)


def _fft2_call(yr, yi, g_stack):
    steps = FFT_N2 // FFT_KB
    return pl.pallas_call(
        _fft2_kernel,
        grid=(BATCH, steps),
        in_specs=[pl.BlockSpec((FFT_KB, FFT_N1, 512), lambda b, j: (b * steps + j, 0, 0)),
                  pl.BlockSpec((FFT_KB, FFT_N1, 512), lambda b, j: (b * steps + j, 0, 0)),
                  pl.BlockSpec((2 * FFT_N1, 2 * FFT_N1), lambda b, j: (0, 0))],
        out_specs=[pl.BlockSpec((FFT_N1, FFT_KB * 512), lambda b, j: (b, j))] * 2,
        out_shape=[jax.ShapeDtypeStruct((BATCH * FFT_N1, FFT_N2 * 512), BF16)] * 2,
        compiler_params=_cparams(2),
        name="fft2",
    )(yr.reshape(BATCH * FFT_N2, FFT_N1, 512), yi.reshape(BATCH * FFT_N2, FFT_N1, 512), g_stack)


def _fft_ctx_kernel(x_ref, f_ref, zr_ref, zi_ref):
    z = jnp.dot(f_ref[...], x_ref[...], preferred_element_type=F32)
    zr_ref[...] = z[:CTX_LEN].astype(BF16)
    zi_ref[...] = z[CTX_LEN:].astype(BF16)


def _fft_ctx_call(uc, f_stack):
    return pl.pallas_call(
        _fft_ctx_kernel,
        grid=(BATCH,),
        in_specs=[pl.BlockSpec((CTX_LEN, 512), lambda b: (2 * NQ + b, 0)),
                  pl.BlockSpec((2 * CTX_LEN, CTX_LEN), lambda b: (0, 0))],
        out_specs=[pl.BlockSpec((CTX_LEN, 512), lambda b: (b, 0))] * 2,
        out_shape=[jax.ShapeDtypeStruct((N_CTX, 512), BF16)] * 2,
        compiler_params=_cparams(1),
        name="fft_ctx",
    )(uc, f_stack)


def _merge_kernel(h_ref, mods_ref, g_ref, *rest, with_ctx):
    lat = [r[...] for r in rest[:4]]
    if with_ctx:
        is_ctx = pl.program_id(0) == NT_LAT
        ya, yb, zr, zi = [jnp.where(is_ctx, c[...], x) for c, x in zip(rest[4:8], lat)]
        wg_ref, wb_ref, wo_ref, cs_ref, o_ref = rest[8:]
    else:
        ya, yb, zr, zi = lat
        wg_ref, wb_ref, wo_ref, cs_ref, o_ref = rest[4:]
    h = h_ref[...]
    m = mods_ref[0]
    hn = _rms_mod(h, g_ref[...], m[3:4], m[4:5]).astype(BF16)
    yc = jnp.concatenate(
        [jnp.dot(jnp.concatenate([zr[:, LANES * g:LANES * (g + 1)], zi[:, LANES * g:LANES * (g + 1)]], axis=1),
                 cs_ref[...], preferred_element_type=F32) for g in range(4)], axis=1).astype(BF16)
    branches = (ya, yb, yc)
    mix = None
    for i in range(3):
        gl = jnp.dot(hn, wg_ref[:, D_MODEL * i:D_MODEL * (i + 1)], preferred_element_type=F32)
        pr = jnp.dot(branches[i], wb_ref[i], preferred_element_type=F32)
        t = jax.nn.sigmoid(gl) * pr
        mix = t if mix is None else mix + t
    o_ref[...] = h + m[5:6] * jnp.dot(mix.astype(BF16), wo_ref[...], preferred_element_type=F32)


def _merge_call(h, mods, g, lat, ctx, wg, wb, wo, cs):
    with_ctx = ctx is not None
    n_tiles = NT_ALL if with_ctx else NT_LAT
    row = lambda i: (i, 0)
    lat_row = lambda i: (jnp.minimum(i, NT_LAT - 1), 0)
    ctx_specs = [_resident((N_CTX, 512))] * 4 if with_ctx else []
    return pl.pallas_call(
        functools.partial(_merge_kernel, with_ctx=with_ctx),
        grid=(n_tiles,),
        in_specs=[pl.BlockSpec((TM, D_MODEL), row),
                  pl.BlockSpec((1, N_MOD, D_MODEL), lambda i: (_mod_set(i), 0, 0)),
                  _resident((1, D_MODEL))]
                 + [pl.BlockSpec((TM, 512), lat_row)] * 4
                 + ctx_specs
                 + [_resident((D_MODEL, 3 * D_MODEL)),
                    _resident((3, 512, D_MODEL)),
                    _resident((D_MODEL, D_MODEL)),
                    _resident((2 * LANES, LANES))],
        out_specs=pl.BlockSpec((TM, D_MODEL), row),
        out_shape=jax.ShapeDtypeStruct((n_tiles * TM, D_MODEL), F32),
        compiler_params=_cparams(1),
        name="merge",
    )(h, mods, g, *lat, *(ctx or ()), wg, wb, wo, cs)


def _dft_cos_sin(n, rows=None, cols=None):
    j = jnp.arange(n if rows is None else rows, dtype=jnp.int32)[:, None]
    k = jnp.arange(n if cols is None else cols, dtype=jnp.int32)[None, :]
    ang = ((j * k) % n).astype(F32) * (2.0 * math.pi / n)
    return jnp.cos(ang), jnp.sin(ang)


def _rope_tables():
    rows = SEQ // GRID_W
    row = jnp.repeat(jnp.arange(rows, dtype=F32), GRID_W)
    col = jnp.tile(jnp.arange(GRID_W, dtype=F32), rows)
    n_freq = HEAD_DIM // 4
    inv = jnp.power(ROPE_THETA, -jnp.arange(n_freq, dtype=F32) / n_freq)
    ang = jnp.concatenate([row[:, None] * inv, col[:, None] * inv], axis=-1)
    cos, sin = jnp.cos(ang), jnp.sin(ang)
    rc = jnp.tile(cos, (1, 4))
    rs = jnp.tile(jnp.concatenate([-sin, sin], axis=-1), (1, 2))
    rc = jnp.concatenate([rc, jnp.ones((TM, LANES), F32)], axis=0)
    rs = jnp.concatenate([rs, jnp.zeros((TM, LANES), F32)], axis=0)
    return rc, rs


def _pad_cols(w, n_groups, width, padded):
    d = w.shape[0]
    w = w.reshape(d, n_groups, width)
    return jnp.pad(w, ((0, 0), (0, 0), (0, padded - width))).reshape(d, n_groups * padded)


def _proj_weight(w_in_l):
    o = 0
    parts = []
    for n_groups, width, padded in ((8, 64, 128), (2, 64, 128), (2, 64, 128), (1, 512, 512),
                                    (1, 512, 512), (4, 128, 256), (1, 512, 512)):
        n = n_groups * width
        parts.append(_pad_cols(w_in_l[:, o:o + n], n_groups, width, padded))
        o += n
    return jnp.concatenate(parts, axis=1).astype(BF16), o


def kernel(x, c, ctx, c_ctx, w_ada, b_ada, norm_g, ffn_wi, ffn_wo, w_in, qk_g, diff_lam, diff_subln_g, w_branch, w_out):
    rc, rs = _rope_tables()
    c128, s128 = _dft_cos_sin(FFT_N2)
    f1 = (jnp.concatenate([c128, -s128], axis=0) * FFT_N2 ** -0.5).astype(BF16)
    c64, s64 = _dft_cos_sin(FFT_N1)
    g2 = (jnp.concatenate([jnp.concatenate([c64, s64], axis=1),
                           jnp.concatenate([-s64, c64], axis=1)], axis=0) * FFT_N1 ** -0.5).astype(BF16)
    twc, tws = _dft_cos_sin(SEQ, rows=FFT_N2, cols=FFT_N1)
    twc = jnp.repeat(twc, LANES, axis=1)
    tws = jnp.repeat(tws, LANES, axis=1)
    cctx, sctx = _dft_cos_sin(CTX_LEN)
    fctx = (jnp.concatenate([cctx, -sctx], axis=0) * CTX_LEN ** -0.5).astype(BF16)
    cs = (jnp.concatenate([c128, s128], axis=0) * LANES ** -0.5).astype(BF16)

    ct = jnp.zeros((D_MODEL, 8), F32).at[:, 0:BATCH].set(c.T).at[:, BATCH].set(c_ctx)
    mods_all = _ada_call(ct, w_ada, b_ada).reshape(DEPTH, 8, N_MOD, D_MODEL)[:, :3]

    h = jnp.concatenate([x.reshape(N_LAT, D_MODEL), ctx.reshape(N_CTX, D_MODEL)], axis=0)
    q_mul = HEAD_DIM ** -0.5 * math.log2(math.e)
    q_scale = jnp.array([q_mul, 1.0, q_mul, 1.0], F32)[:, None]
    for l in range(DEPTH):
        last = l == DEPTH - 1
        mods = mods_all[l]
        ng = norm_g[l].reshape(3, 1, D_MODEL)
        wi = [ffn_wi[l, k].astype(BF16) for k in range(2)]
        wo = [ffn_wo[l, k].astype(BF16) for k in range(2)]
        w_proj, gate_off = _proj_weight(w_in[l])
        w_gate = w_in[l][:, gate_off:].astype(BF16)
        gt = jnp.tile(qk_g[l], (1, 2)) * q_scale
        lam_init = 0.8 - 0.6 * math.exp(-0.3 * l)
        subln = diff_subln_g[l][None, :]

        h = _ffn_call(h, mods, ng[0], wi[0], wo[0], k=0, n_tiles=NT_ALL)
        qa, ka, va, qb, kb, vb, uc = _proj_call(h, mods, ng[1], w_proj, gt, rc, rs)
        ya = _attn_a_call(qa, ka, va, ctx_queries=False)
        yb = _attn_b_call(qb, kb, vb, diff_lam[l], subln, ctx_queries=False, lam_init=lam_init)
        yr, yi = _fft1_call(uc.reshape(R_ALL * 512 // (FFT_N1 * 512), FFT_N1 * 512), f1, twc, tws)
        zr, zi = _fft2_call(yr, yi, g2)
        lat = (ya, yb, zr.reshape(N_LAT, 512), zi.reshape(N_LAT, 512))
        ctx_parts = None
        if not last:
            ctx_parts = (_attn_a_call(qa, ka, va, ctx_queries=True),
                         _attn_b_call(qb, kb, vb, diff_lam[l], subln, ctx_queries=True, lam_init=lam_init),
                         *_fft_ctx_call(uc, fctx))
        h = _merge_call(h, mods, ng[1], lat, ctx_parts, w_gate, w_branch[l].astype(BF16), w_out[l].astype(BF16), cs)
        n_tiles = NT_LAT if last else NT_ALL
        h = _ffn_call(h, mods, ng[2], wi[1], wo[1], k=2, n_tiles=n_tiles)
    return h.reshape(BATCH, SEQ, D_MODEL)
```

```python
import functools
import math

import jax
import jax.numpy as jnp
from jax import lax
from jax.experimental import pallas as pl
from jax.experimental.pallas import tpu as pltpu

F32 = jnp.float32
BF16 = jnp.bfloat16

D_MODEL = 1024
BATCH = 2
SEQ = 8192
DEPTH = 2
GRID_W = 64
CTX_LEN = 256
HEAD_DIM = 64
D_FF = 2816
N_MOD = 9
ROPE_THETA = 10000.0
EPS = 1e-6

LANES = 128
N_LAT = BATCH * SEQ
N_CTX = BATCH * CTX_LEN
R_ALL = N_LAT + N_CTX
TM = 512
NT_LAT = N_LAT // TM
NT_ALL = R_ALL // TM
TILES_PER_BATCH = SEQ // TM
FF_CHUNK = 256
TQ = 256
NQ = SEQ // TQ
TK = 1024
ADA_TN = 1152
FFT_N1 = 64
FFT_N2 = 128
FFT_TN = 4096
FFT_KB = 16
VMEM_LIMIT = 56 * 1024 * 1024


def _cparams(n_axes):
    return pltpu.CompilerParams(dimension_semantics=("arbitrary",) * n_axes,
                                vmem_limit_bytes=VMEM_LIMIT)


def _resident(shape):
    nd = len(shape)
    return pl.BlockSpec(shape, lambda *_: (0,) * nd, pipeline_mode=pl.Buffered(1))


def _mod_set(i):
    return jnp.minimum(i // TILES_PER_BATCH, 2)


def _rms_mod(h, g, shift, scale):
    y = h * lax.rsqrt(jnp.mean(h * h, axis=-1, keepdims=True) + EPS) * g
    return y * (1.0 + scale) + shift


def _ada_kernel(ct_ref, w_ref, b_ref, o_ref):
    ct = ct_ref[...]
    s = ct * jax.nn.sigmoid(ct)
    w = w_ref[0]
    rows = [jnp.sum(w * s[:, r:r + 1], axis=0, keepdims=True) for r in range(3)]
    rows.append(jnp.zeros((5, ADA_TN), F32))
    o_ref[0] = jnp.concatenate(rows, axis=0) + b_ref[0]


def _ada_call(ct, w_ada, b_ada):
    n_col = N_MOD * D_MODEL
    return pl.pallas_call(
        _ada_kernel,
        grid=(DEPTH, n_col // ADA_TN),
        in_specs=[pl.BlockSpec((D_MODEL, 8), lambda l, j: (0, 0)),
                  pl.BlockSpec((1, D_MODEL, ADA_TN), lambda l, j: (l, 0, j)),
                  pl.BlockSpec((1, 1, ADA_TN), lambda l, j: (l, 0, j))],
        out_specs=pl.BlockSpec((1, 8, ADA_TN), lambda l, j: (l, 0, j)),
        out_shape=jax.ShapeDtypeStruct((DEPTH, 8, n_col), F32),
        compiler_params=_cparams(2),
        name="ada",
    )(ct, w_ada, b_ada.reshape(DEPTH, 1, n_col))


def _ffn_kernel(h_ref, mods_ref, g_ref, wi_ref, wo_ref, o_ref, *, k):
    h = h_ref[...]
    m = mods_ref[0]
    hn = _rms_mod(h, g_ref[...], m[3 * k:3 * k + 1], m[3 * k + 1:3 * k + 2]).astype(BF16)
    acc = None
    for lo in range(0, D_FF, FF_CHUNK):
        cols = slice(lo, min(lo + FF_CHUNK, D_FF))
        g = jnp.dot(hn, wi_ref[:, cols], preferred_element_type=F32)
        u = jnp.dot(hn, wi_ref[:, D_FF + cols.start:D_FF + cols.stop], preferred_element_type=F32)
        a = (g * jax.nn.sigmoid(g) * u).astype(BF16)
        o = jnp.dot(a, wo_ref[cols, :], preferred_element_type=F32)
        acc = o if acc is None else acc + o
    o_ref[...] = h + 0.5 * m[3 * k + 2:3 * k + 3] * acc


def _ffn_call(h, mods, g, wi, wo, *, k, n_tiles):
    rows = n_tiles * TM
    return pl.pallas_call(
        functools.partial(_ffn_kernel, k=k),
        grid=(n_tiles,),
        in_specs=[pl.BlockSpec((TM, D_MODEL), lambda i: (i, 0)),
                  pl.BlockSpec((1, N_MOD, D_MODEL), lambda i: (_mod_set(i), 0, 0)),
                  _resident((1, D_MODEL)),
                  _resident((D_MODEL, 2 * D_FF)),
                  _resident((D_FF, D_MODEL))],
        out_specs=pl.BlockSpec((TM, D_MODEL), lambda i: (i, 0)),
        out_shape=jax.ShapeDtypeStruct((rows, D_MODEL), F32),
        compiler_params=_cparams(1),
        name=f"ffn{k}",
    )(h, mods, g, wi, wo)


N_PROJ = 2816


def _proj_kernel(h_ref, mods_ref, g_ref, w_ref, gt_ref, rc_ref, rs_ref, avg_ref,
                 qa_ref, ka_ref, va_ref, qb_ref, kb_ref, vb_ref, uc_ref):
    h = h_ref[...]
    m = mods_ref[0]
    hn = _rms_mod(h, g_ref[...], m[3:4], m[4:5]).astype(BF16)
    rc = rc_ref[...]
    rs = rs_ref[...]
    gt = gt_ref[...]
    lane = lax.broadcasted_iota(jnp.int32, (TM, LANES), 1)
    lo = lane < HEAD_DIM
    first_half = (lane % HEAD_DIM) < HEAD_DIM // 2

    def rope(y):
        swapped = jnp.where(first_half, pltpu.roll(y, 96, 1), pltpu.roll(y, 32, 1))
        return y * rc + swapped * rs

    def project(lo_col, hi_col):
        return jnp.dot(hn, w_ref[:, lo_col:hi_col], preferred_element_type=F32)

    def slabs(z):
        return [z[:, LANES * j:LANES * (j + 1)] for j in range(z.shape[1] // LANES)]

    def qk_norm(z, g):
        wd = z.shape[1]
        ms = jnp.dot((z * z).astype(BF16), avg_ref[:wd, :wd], preferred_element_type=F32)
        y = z * lax.rsqrt(ms + EPS)
        return [rope(s * g) for s in slabs(y)]

    def one_per_slab(pair, fill):
        return [jnp.where(lo, pair, fill), jnp.where(lo, pltpu.roll(pair, HEAD_DIM, 1), fill)]

    def store(ref, parts):
        ref[...] = jnp.concatenate(parts, axis=1).astype(BF16)

    store(qa_ref, [s for pair in qk_norm(project(0, 512), gt[0:1]) for s in one_per_slab(pair, 0.0)])
    z = project(512, 768)
    store(ka_ref, one_per_slab(qk_norm(z[:, :LANES], gt[1:2])[0], 0.0))
    store(va_ref, one_per_slab(z[:, LANES:], 1.0))
    store(qb_ref, qk_norm(project(768, 1280), gt[2:3]))
    store(kb_ref, qk_norm(project(1280, 1792), gt[3:4]))
    ones = jnp.ones((TM, LANES), F32)
    store(vb_ref, [part for v in slabs(project(1792, 2304)) for part in (v, ones)])
    store(uc_ref, [project(2304, N_PROJ)])


def _proj_call(h, mods, g, w, gt, rc, rs, avg):
    widths = (1024, 256, 256, 512, 512, 1024, 512)
    return pl.pallas_call(
        _proj_kernel,
        grid=(NT_ALL,),
        in_specs=[pl.BlockSpec((TM, D_MODEL), lambda i: (i, 0)),
                  pl.BlockSpec((1, N_MOD, D_MODEL), lambda i: (_mod_set(i), 0, 0)),
                  _resident((1, D_MODEL)),
                  _resident((D_MODEL, N_PROJ)),
                  _resident((4, LANES)),
                  pl.BlockSpec((TM, LANES), lambda i: (jnp.where(i < NT_LAT, i % TILES_PER_BATCH, TILES_PER_BATCH), 0)),
                  pl.BlockSpec((TM, LANES), lambda i: (jnp.where(i < NT_LAT, i % TILES_PER_BATCH, TILES_PER_BATCH), 0)),
                  _resident((512, 512))],
        out_specs=[pl.BlockSpec((TM, wd), lambda i: (i, 0)) for wd in widths],
        out_shape=[jax.ShapeDtypeStruct((R_ALL, wd), BF16) for wd in widths],
        compiler_params=_cparams(1),
        name="proj",
    )(h, mods, g, w, gt, rc, rs, avg)


def _attn_pipeline(chunks, qs, k_of, v_of, s_ref, p_ref, al_ref, m_ref, acc_ref):
    width = lambda c: c[2]

    def qk(i):
        for st, q in enumerate(qs):
            s_ref[st, i % 2, :, :width(chunks[i])] = lax.dot_general(
                q, k_of(chunks[i], st), (((1,), (1,)), ((), ())), preferred_element_type=F32)

    def softmax(i):
        for st in range(len(qs)):
            s = s_ref[st, i % 2, :, :width(chunks[i])]
            m_old = m_ref[st]
            m_new = jnp.maximum(m_old, jnp.max(s, axis=-1, keepdims=True))
            al_ref[st, i % 2] = jnp.exp2(m_old - m_new)
            for j in range(width(chunks[i]) // LANES):
                cols = slice(LANES * j, LANES * (j + 1))
                p_ref[st, i % 2, :, cols] = jnp.exp2(s[:, cols] - m_new).astype(BF16)
            m_ref[st] = m_new

    def pv(i):
        for st in range(len(qs)):
            o = jnp.dot(p_ref[st, i % 2, :, :width(chunks[i])], v_of(chunks[i], st), preferred_element_type=F32)
            al = al_ref[st, i % 2]
            for j in range(acc_ref.shape[2] // LANES):
                cols = slice(LANES * j, LANES * (j + 1))
                acc_ref[st, :, cols] = al * acc_ref[st, :, cols] + o[:, cols]

    m_ref[...] = jnp.full_like(m_ref, -1e30)
    acc_ref[...] = jnp.zeros_like(acc_ref)
    n = len(chunks)
    qk(0)
    if n > 1:
        qk(1)
    softmax(0)
    for i in range(n - 2):
        pv(i)
        qk(i + 2)
        softmax(i + 1)
    if n > 1:
        pv(n - 2)
        softmax(n - 1)
    pv(n - 1)


def _chunk_list(ctx_queries):
    ctx = (True, 0, CTX_LEN)
    return [ctx] if ctx_queries else [(False, r, TK) for r in range(0, SEQ, TK)] + [ctx]


def _chunk_rows(c):
    return pl.ds(c[1], c[2])


def _split_attn_refs(refs, ctx_queries, n_extra):
    q_ref, kc_ref, vc_ref = refs[:3]
    kl_ref, vl_ref = (None, None) if ctx_queries else refs[3:5]
    rest = refs[3 if ctx_queries else 5:]
    return q_ref, kc_ref, vc_ref, kl_ref, vl_ref, rest[:n_extra], rest[n_extra], rest[n_extra + 1:]


def _attn_a_kernel(*refs, ctx_queries):
    q_ref, kc_ref, vc_ref, kl_ref, vl_ref, _, o_ref, scratch = _split_attn_refs(refs, ctx_queries, 0)
    q4 = jnp.concatenate([q_ref[:, LANES * j:LANES * (j + 1)] for j in range(4)], axis=0)
    k_of = lambda c, st: (kc_ref if c[0] else kl_ref)[_chunk_rows(c), :]
    v_of = lambda c, st: (vc_ref if c[0] else vl_ref)[_chunk_rows(c), :]
    _attn_pipeline(_chunk_list(ctx_queries), [q4], k_of, v_of, *scratch)
    acc = scratch[-1][0]
    o = acc / pltpu.roll(acc, HEAD_DIM, 1)
    lo = lax.broadcasted_iota(jnp.int32, (TQ, LANES), 1) < HEAD_DIM
    pairs = [jnp.where(lo, o[2 * j * TQ:(2 * j + 1) * TQ], pltpu.roll(o[(2 * j + 1) * TQ:(2 * j + 2) * TQ], HEAD_DIM, 1))
             for j in range(2)]
    o_ref[...] = jnp.concatenate(pairs, axis=1).astype(BF16)


def _attn_scratch(n_streams, rows, n_v):
    return [pltpu.VMEM((n_streams, 2, rows, TK), F32),
            pltpu.VMEM((n_streams, 2, rows, TK), BF16),
            pltpu.VMEM((n_streams, 2, rows, LANES), F32),
            pltpu.VMEM((n_streams, rows, LANES), F32),
            pltpu.VMEM((n_streams, rows, n_v), F32)]


def _attn_specs(ctx_queries, q_w, k_w, v_w):
    ctx_blk = lambda b, g, qi: (2 * NQ + b, g)
    q_blk = ctx_blk if ctx_queries else (lambda b, g, qi: (b * NQ + qi, g))
    o_blk = (lambda b, g, qi: (b, g)) if ctx_queries else q_blk
    specs = [pl.BlockSpec((TQ, q_w), q_blk),
             pl.BlockSpec((CTX_LEN, k_w), ctx_blk),
             pl.BlockSpec((CTX_LEN, v_w), ctx_blk)]
    if not ctx_queries:
        lat_blk = lambda b, g, qi: (b, g)
        specs += [pl.BlockSpec((SEQ, k_w), lat_blk, pipeline_mode=pl.Buffered(1)),
                  pl.BlockSpec((SEQ, v_w), lat_blk, pipeline_mode=pl.Buffered(1))]
    return specs, pl.BlockSpec((TQ, 256), o_blk)


def _attn_a_call(qa, ka, va, *, ctx_queries):
    in_specs, out_spec = _attn_specs(ctx_queries, 512, LANES, LANES)
    kv = (ka, va) if ctx_queries else (ka, va, ka, va)
    return pl.pallas_call(
        functools.partial(_attn_a_kernel, ctx_queries=ctx_queries),
        grid=(BATCH, 2, 1 if ctx_queries else NQ),
        in_specs=in_specs,
        out_specs=out_spec,
        out_shape=jax.ShapeDtypeStruct((N_CTX if ctx_queries else N_LAT, 512), BF16),
        scratch_shapes=_attn_scratch(1, 4 * TQ, LANES),
        compiler_params=_cparams(3),
        name="attn_a_ctx" if ctx_queries else "attn_a",
    )(qa, *kv)


def _attn_b_kernel(*refs, ctx_queries, lam_init):
    q_ref, kc_ref, vc_ref, kl_ref, vl_ref, (dl_ref, sg_ref), o_ref, scratch = _split_attn_refs(refs, ctx_queries, 2)
    lo = lax.broadcasted_iota(jnp.int32, (TQ, LANES), 1) < HEAD_DIM
    zero = jnp.zeros((TQ, LANES), BF16)
    q2 = []
    for hh in range(2):
        qh = q_ref[:, LANES * hh:LANES * (hh + 1)]
        q2.append(jnp.concatenate([jnp.where(lo, qh, zero), jnp.where(lo, zero, qh)], axis=0))
    k_of = lambda c, hh: (kc_ref if c[0] else kl_ref)[_chunk_rows(c), LANES * hh:LANES * (hh + 1)]
    v_of = lambda c, hh: (vc_ref if c[0] else vl_ref)[_chunk_rows(c), 256 * hh:256 * (hh + 1)]
    _attn_pipeline(_chunk_list(ctx_queries), q2, k_of, v_of, *scratch)

    lp = dl_ref[...]
    lam = (jnp.exp(jnp.sum(lp[0:1] * lp[1:2], axis=-1, keepdims=True))
           - jnp.exp(jnp.sum(lp[2:3] * lp[3:4], axis=-1, keepdims=True)) + lam_init)
    outs = []
    for hh in range(2):
        acc = scratch[-1][hh]
        o = acc[:, :LANES] / acc[:, LANES:]
        d = o[:TQ] - lam * o[TQ:]
        y = d * lax.rsqrt(jnp.mean(d * d, axis=-1, keepdims=True) + EPS) * sg_ref[...]
        outs.append(y * (1.0 - lam_init))
    o_ref[...] = jnp.concatenate(outs, axis=1).astype(BF16)


def _attn_b_call(qb, kb, vb, diff_lam_l, subln_g, *, ctx_queries, lam_init):
    in_specs, out_spec = _attn_specs(ctx_queries, 256, 256, 512)
    in_specs += [pl.BlockSpec((4, HEAD_DIM), lambda b, hp, qi: (0, 0)),
                 pl.BlockSpec((1, LANES), lambda b, hp, qi: (0, 0))]
    kv = (kb, vb) if ctx_queries else (kb, vb, kb, vb)
    return pl.pallas_call(
        functools.partial(_attn_b_kernel, ctx_queries=ctx_queries, lam_init=lam_init),
        grid=(BATCH, 2, 1 if ctx_queries else NQ),
        in_specs=in_specs,
        out_specs=out_spec,
        out_shape=jax.ShapeDtypeStruct((N_CTX if ctx_queries else N_LAT, 512), BF16),
        scratch_shapes=_attn_scratch(2, 2 * TQ, 256),
        compiler_params=_cparams(3),
        name="attn_b_ctx" if ctx_queries else "attn_b",
    )(qb, *kv, diff_lam_l, subln_g)


def _fft1_kernel(x_ref, f_ref, tc_ref, ts_ref, yr_ref, yi_ref):
    y = jnp.dot(f_ref[...], x_ref[...], preferred_element_type=F32)
    for t in range(FFT_TN // 512):
        c = jnp.concatenate([tc_ref[:, LANES * t:LANES * (t + 1)]] * 4, axis=1)
        s = jnp.concatenate([ts_ref[:, LANES * t:LANES * (t + 1)]] * 4, axis=1)
        a = y[:FFT_N2, 512 * t:512 * (t + 1)]
        b = y[FFT_N2:, 512 * t:512 * (t + 1)]
        yr_ref[:, 512 * t:512 * (t + 1)] = (a * c + b * s).astype(BF16)
        yi_ref[:, 512 * t:512 * (t + 1)] = (b * c - a * s).astype(BF16)


def _fft1_call(uc_view, f_stack, tc, ts):
    n_col = FFT_N1 * 512
    return pl.pallas_call(
        _fft1_kernel,
        grid=(BATCH, n_col // FFT_TN),
        in_specs=[pl.BlockSpec((FFT_N2, FFT_TN), lambda b, j: (b, j)),
                  pl.BlockSpec((2 * FFT_N2, FFT_N2), lambda b, j: (0, 0)),
                  pl.BlockSpec((FFT_N2, FFT_TN // 4), lambda b, j: (0, j)),
                  pl.BlockSpec((FFT_N2, FFT_TN // 4), lambda b, j: (0, j))],
        out_specs=[pl.BlockSpec((FFT_N2, FFT_TN), lambda b, j: (b, j))] * 2,
        out_shape=[jax.ShapeDtypeStruct((BATCH * FFT_N2, n_col), BF16)] * 2,
        compiler_params=_cparams(2),
        name="fft1",
    )(uc_view, f_stack, tc, ts)


def _fft2_kernel(yr_ref, yi_ref, g_ref, zr_ref, zi_ref):
    for t in range(FFT_KB):
        ys = jnp.concatenate([yr_ref[t], yi_ref[t]], axis=0)
        z = jnp.dot(g_ref[...], ys, preferred_element_type=F32)
        zr_ref[:, 512 * t:512 * (t + 1)] = z[:FFT_N1].astype(BF16)
        zi_ref[:, 512 * t:512 * (t + 1)] = z[FFT_N1:].astype(BF16)


def _fft2_call(yr, yi, g_stack):
    steps = FFT_N2 // FFT_KB
    return pl.pallas_call(
        _fft2_kernel,
        grid=(BATCH, steps),
        in_specs=[pl.BlockSpec((FFT_KB, FFT_N1, 512), lambda b, j: (b * steps + j, 0, 0)),
                  pl.BlockSpec((FFT_KB, FFT_N1, 512), lambda b, j: (b * steps + j, 0, 0)),
                  pl.BlockSpec((2 * FFT_N1, 2 * FFT_N1), lambda b, j: (0, 0))],
        out_specs=[pl.BlockSpec((FFT_N1, FFT_KB * 512), lambda b, j: (b, j))] * 2,
        out_shape=[jax.ShapeDtypeStruct((BATCH * FFT_N1, FFT_N2 * 512), BF16)] * 2,
        compiler_params=_cparams(2),
        name="fft2",
    )(yr.reshape(BATCH * FFT_N2, FFT_N1, 512), yi.reshape(BATCH * FFT_N2, FFT_N1, 512), g_stack)


def _fft_ctx_kernel(x_ref, f_ref, zr_ref, zi_ref):
    z = jnp.dot(f_ref[...], x_ref[...], preferred_element_type=F32)
    zr_ref[...] = z[:CTX_LEN].astype(BF16)
    zi_ref[...] = z[CTX_LEN:].astype(BF16)


def _fft_ctx_call(uc, f_stack):
    return pl.pallas_call(
        _fft_ctx_kernel,
        grid=(BATCH,),
        in_specs=[pl.BlockSpec((CTX_LEN, 512), lambda b: (2 * NQ + b, 0)),
                  pl.BlockSpec((2 * CTX_LEN, CTX_LEN), lambda b: (0, 0))],
        out_specs=[pl.BlockSpec((CTX_LEN, 512), lambda b: (b, 0))] * 2,
        out_shape=[jax.ShapeDtypeStruct((N_CTX, 512), BF16)] * 2,
        compiler_params=_cparams(1),
        name="fft_ctx",
    )(uc, f_stack)


def _merge_kernel(h_ref, mods_ref, g_ref, *rest, with_ctx):
    lat = [r[...] for r in rest[:4]]
    if with_ctx:
        is_ctx = pl.program_id(0) == NT_LAT
        ya, yb, zr, zi = [jnp.where(is_ctx, c[...], x) for c, x in zip(rest[4:8], lat)]
        wg_ref, wb_ref, wo_ref, cs_ref, o_ref = rest[8:]
    else:
        ya, yb, zr, zi = lat
        wg_ref, wb_ref, wo_ref, cs_ref, o_ref = rest[4:]
    h = h_ref[...]
    m = mods_ref[0]
    hn = _rms_mod(h, g_ref[...], m[3:4], m[4:5]).astype(BF16)
    yc = jnp.concatenate(
        [jnp.dot(jnp.concatenate([zr[:, LANES * g:LANES * (g + 1)], zi[:, LANES * g:LANES * (g + 1)]], axis=1),
                 cs_ref[...], preferred_element_type=F32) for g in range(4)], axis=1).astype(BF16)
    branches = (ya, yb, yc)
    mix = None
    for i in range(3):
        gl = jnp.dot(hn, wg_ref[:, D_MODEL * i:D_MODEL * (i + 1)], preferred_element_type=F32)
        pr = jnp.dot(branches[i], wb_ref[i], preferred_element_type=F32)
        t = jax.nn.sigmoid(gl) * pr
        mix = t if mix is None else mix + t
    o_ref[...] = h + m[5:6] * jnp.dot(mix.astype(BF16), wo_ref[...], preferred_element_type=F32)


def _merge_call(h, mods, g, lat, ctx, wg, wb, wo, cs):
    with_ctx = ctx is not None
    n_tiles = NT_ALL if with_ctx else NT_LAT
    row = lambda i: (i, 0)
    lat_row = lambda i: (jnp.minimum(i, NT_LAT - 1), 0)
    ctx_specs = [_resident((N_CTX, 512))] * 4 if with_ctx else []
    return pl.pallas_call(
        functools.partial(_merge_kernel, with_ctx=with_ctx),
        grid=(n_tiles,),
        in_specs=[pl.BlockSpec((TM, D_MODEL), row),
                  pl.BlockSpec((1, N_MOD, D_MODEL), lambda i: (_mod_set(i), 0, 0)),
                  _resident((1, D_MODEL))]
                 + [pl.BlockSpec((TM, 512), lat_row)] * 4
                 + ctx_specs
                 + [_resident((D_MODEL, 3 * D_MODEL)),
                    _resident((3, 512, D_MODEL)),
                    _resident((D_MODEL, D_MODEL)),
                    _resident((2 * LANES, LANES))],
        out_specs=pl.BlockSpec((TM, D_MODEL), row),
        out_shape=jax.ShapeDtypeStruct((n_tiles * TM, D_MODEL), F32),
        compiler_params=_cparams(1),
        name="merge",
    )(h, mods, g, *lat, *(ctx or ()), wg, wb, wo, cs)


def _dft_cos_sin(n, rows=None, cols=None):
    j = jnp.arange(n if rows is None else rows, dtype=jnp.int32)[:, None]
    k = jnp.arange(n if cols is None else cols, dtype=jnp.int32)[None, :]
    ang = ((j * k) % n).astype(F32) * (2.0 * math.pi / n)
    return jnp.cos(ang), jnp.sin(ang)


def _rope_tables():
    rows = SEQ // GRID_W
    row = jnp.repeat(jnp.arange(rows, dtype=F32), GRID_W)
    col = jnp.tile(jnp.arange(GRID_W, dtype=F32), rows)
    n_freq = HEAD_DIM // 4
    inv = jnp.power(ROPE_THETA, -jnp.arange(n_freq, dtype=F32) / n_freq)
    ang = jnp.concatenate([row[:, None] * inv, col[:, None] * inv], axis=-1)
    cos, sin = jnp.cos(ang), jnp.sin(ang)
    rc = jnp.tile(cos, (1, 4))
    rs = jnp.tile(jnp.concatenate([-sin, sin], axis=-1), (1, 2))
    rc = jnp.concatenate([rc, jnp.ones((TM, LANES), F32)], axis=0)
    rs = jnp.concatenate([rs, jnp.zeros((TM, LANES), F32)], axis=0)
    return rc, rs


def kernel(x, c, ctx, c_ctx, w_ada, b_ada, norm_g, ffn_wi, ffn_wo, w_in, qk_g, diff_lam, diff_subln_g, w_branch, w_out):
    rc, rs = _rope_tables()
    c128, s128 = _dft_cos_sin(FFT_N2)
    f1 = (jnp.concatenate([c128, -s128], axis=0) * FFT_N2 ** -0.5).astype(BF16)
    c64, s64 = _dft_cos_sin(FFT_N1)
    g2 = (jnp.concatenate([jnp.concatenate([c64, s64], axis=1),
                           jnp.concatenate([-s64, c64], axis=1)], axis=0) * FFT_N1 ** -0.5).astype(BF16)
    twc, tws = _dft_cos_sin(SEQ, rows=FFT_N2, cols=FFT_N1)
    twc = jnp.repeat(twc, LANES, axis=1)
    tws = jnp.repeat(tws, LANES, axis=1)
    cctx, sctx = _dft_cos_sin(CTX_LEN)
    fctx = (jnp.concatenate([cctx, -sctx], axis=0) * CTX_LEN ** -0.5).astype(BF16)
    cs = (jnp.concatenate([c128, s128], axis=0) * LANES ** -0.5).astype(BF16)
    avg = jnp.kron(jnp.eye(8, dtype=F32), jnp.full((HEAD_DIM, HEAD_DIM), 1.0 / HEAD_DIM, F32)).astype(BF16)

    ct = jnp.zeros((D_MODEL, 8), F32).at[:, 0:BATCH].set(c.T).at[:, BATCH].set(c_ctx)
    mods_all = _ada_call(ct, w_ada, b_ada).reshape(DEPTH, 8, N_MOD, D_MODEL)[:, :3]

    h = jnp.concatenate([x.reshape(N_LAT, D_MODEL), ctx.reshape(N_CTX, D_MODEL)], axis=0)
    q_mul = HEAD_DIM ** -0.5 * math.log2(math.e)
    q_scale = jnp.array([q_mul, 1.0, q_mul, 1.0], F32)[:, None]
    for l in range(DEPTH):
        last = l == DEPTH - 1
        mods = mods_all[l]
        ng = norm_g[l].reshape(3, 1, D_MODEL)
        wi = [ffn_wi[l, k].astype(BF16) for k in range(2)]
        wo = [ffn_wo[l, k].astype(BF16) for k in range(2)]
        w_proj = w_in[l][:, :N_PROJ].astype(BF16)
        w_gate = w_in[l][:, N_PROJ:].astype(BF16)
        gt = jnp.tile(qk_g[l], (1, 2)) * q_scale
        lam_init = 0.8 - 0.6 * math.exp(-0.3 * l)
        subln = diff_subln_g[l][None, :]

        h = _ffn_call(h, mods, ng[0], wi[0], wo[0], k=0, n_tiles=NT_ALL)
        qa, ka, va, qb, kb, vb, uc = _proj_call(h, mods, ng[1], w_proj, gt, rc, rs, avg)
        ya = _attn_a_call(qa, ka, va, ctx_queries=False)
        yb = _attn_b_call(qb, kb, vb, diff_lam[l], subln, ctx_queries=False, lam_init=lam_init)
        yr, yi = _fft1_call(uc.reshape(R_ALL * 512 // (FFT_N1 * 512), FFT_N1 * 512), f1, twc, tws)
        zr, zi = _fft2_call(yr, yi, g2)
        lat = (ya, yb, zr.reshape(N_LAT, 512), zi.reshape(N_LAT, 512))
        ctx_parts = None
        if not last:
            ctx_parts = (_attn_a_call(qa, ka, va, ctx_queries=True),
                         _attn_b_call(qb, kb, vb, diff_lam[l], subln, ctx_queries=True, lam_init=lam_init),
                         *_fft_ctx_call(uc, fctx))
        h = _merge_call(h, mods, ng[1], lat, ctx_parts, w_gate, w_branch[l].astype(BF16), w_out[l].astype(BF16), cs)
        n_tiles = NT_LAT if last else NT_ALL
        h = _ffn_call(h, mods, ng[2], wi[1], wo[1], k=2, n_tiles=n_tiles)
    return h.reshape(BATCH, SEQ, D_MODEL)
```

```python
import functools
import math

import jax
import jax.numpy as jnp
from jax import lax
from jax.experimental import pallas as pl
from jax.experimental.pallas import tpu as pltpu

F32 = jnp.float32
BF16 = jnp.bfloat16

D_MODEL = 1024
BATCH = 2
SEQ = 8192
DEPTH = 2
GRID_W = 64
CTX_LEN = 256
HEAD_DIM = 64
D_FF = 2816
N_MOD = 9
ROPE_THETA = 10000.0
EPS = 1e-6

LANES = 128
N_LAT = BATCH * SEQ
N_CTX = BATCH * CTX_LEN
R_ALL = N_LAT + N_CTX
TM = 512
NT_LAT = N_LAT // TM
NT_ALL = R_ALL // TM
TILES_PER_BATCH = SEQ // TM
FF_CHUNK = 256
TQ = 512
NQ = SEQ // TQ
CTX_BLK0 = N_LAT // CTX_LEN
TK = 1024
ADA_TN = 1152
FFT_N1 = 64
FFT_N2 = 128
FFT_TN = 4096
FFT_KB = 16
VMEM_LIMIT = 56 * 1024 * 1024


def _cparams(n_axes):
    return pltpu.CompilerParams(dimension_semantics=("arbitrary",) * n_axes,
                                vmem_limit_bytes=VMEM_LIMIT)


def _resident(shape):
    nd = len(shape)
    return pl.BlockSpec(shape, lambda *_: (0,) * nd, pipeline_mode=pl.Buffered(1))


def _mod_set(i):
    return jnp.minimum(i // TILES_PER_BATCH, 2)


def _rms_mod(h, g, shift, scale):
    y = h * lax.rsqrt(jnp.mean(h * h, axis=-1, keepdims=True) + EPS) * g
    return y * (1.0 + scale) + shift


def _ada_kernel(ct_ref, w_ref, b_ref, o_ref):
    ct = ct_ref[...]
    s = ct * jax.nn.sigmoid(ct)
    w = w_ref[0]
    rows = [jnp.sum(w * s[:, r:r + 1], axis=0, keepdims=True) for r in range(3)]
    rows.append(jnp.zeros((5, ADA_TN), F32))
    o_ref[0] = jnp.concatenate(rows, axis=0) + b_ref[0]


def _ada_call(ct, w_ada, b_ada):
    n_col = N_MOD * D_MODEL
    return pl.pallas_call(
        _ada_kernel,
        grid=(DEPTH, n_col // ADA_TN),
        in_specs=[pl.BlockSpec((D_MODEL, 8), lambda l, j: (0, 0)),
                  pl.BlockSpec((1, D_MODEL, ADA_TN), lambda l, j: (l, 0, j)),
                  pl.BlockSpec((1, 1, ADA_TN), lambda l, j: (l, 0, j))],
        out_specs=pl.BlockSpec((1, 8, ADA_TN), lambda l, j: (l, 0, j)),
        out_shape=jax.ShapeDtypeStruct((DEPTH, 8, n_col), F32),
        compiler_params=_cparams(2),
        name="ada",
    )(ct, w_ada, b_ada.reshape(DEPTH, 1, n_col))


def _ffn_kernel(*refs, k, split_rows):
    if split_rows:
        h_ref, hc_ref, mods_ref, g_ref, wi_ref, wo_ref, o_ref = refs
        h = jnp.where(pl.program_id(0) == NT_LAT, hc_ref[...], h_ref[...])
    else:
        h_ref, mods_ref, g_ref, wi_ref, wo_ref, o_ref = refs
        h = h_ref[...]
    m = mods_ref[0]
    hn = _rms_mod(h, g_ref[...], m[3 * k:3 * k + 1], m[3 * k + 1:3 * k + 2]).astype(BF16)
    acc = None
    for lo in range(0, D_FF, FF_CHUNK):
        cols = slice(lo, min(lo + FF_CHUNK, D_FF))
        g = jnp.dot(hn, wi_ref[:, cols], preferred_element_type=F32)
        u = jnp.dot(hn, wi_ref[:, D_FF + cols.start:D_FF + cols.stop], preferred_element_type=F32)
        a = (g * jax.nn.sigmoid(g) * u).astype(BF16)
        o = jnp.dot(a, wo_ref[cols, :], preferred_element_type=F32)
        acc = o if acc is None else acc + o
    o_ref[...] = h + 0.5 * m[3 * k + 2:3 * k + 3] * acc


def _ffn_call(h, mods, g, wi, wo, *, k, n_tiles, h_ctx=None):
    rows = n_tiles * TM
    split_rows = h_ctx is not None
    if split_rows:
        row_specs = [pl.BlockSpec((TM, D_MODEL), lambda i: (jnp.minimum(i, NT_LAT - 1), 0)),
                     _resident((N_CTX, D_MODEL))]
    else:
        row_specs = [pl.BlockSpec((TM, D_MODEL), lambda i: (i, 0))]
    return pl.pallas_call(
        functools.partial(_ffn_kernel, k=k, split_rows=split_rows),
        grid=(n_tiles,),
        in_specs=row_specs
                 + [pl.BlockSpec((1, N_MOD, D_MODEL), lambda i: (_mod_set(i), 0, 0)),
                    _resident((1, D_MODEL)),
                    _resident((D_MODEL, 2 * D_FF)),
                    _resident((D_FF, D_MODEL))],
        out_specs=pl.BlockSpec((TM, D_MODEL), lambda i: (i, 0)),
        out_shape=jax.ShapeDtypeStruct((rows, D_MODEL), F32),
        compiler_params=_cparams(1),
        name=f"ffn{k}",
    )(h, *((h_ctx,) if split_rows else ()), mods, g, wi, wo)


N_PROJ = 2816


def _proj_kernel(h_ref, mods_ref, g_ref, w_ref, gt_ref, rc_ref, rs_ref, avg_ref,
                 qa_ref, ka_ref, va_ref, qb_ref, kb_ref, vb_ref, uc_ref):
    h = h_ref[...]
    m = mods_ref[0]
    hn = _rms_mod(h, g_ref[...], m[3:4], m[4:5]).astype(BF16)
    rc = rc_ref[...]
    rs = rs_ref[...]
    gt = gt_ref[...]
    lane = lax.broadcasted_iota(jnp.int32, (TM, LANES), 1)
    lo = lane < HEAD_DIM
    first_half = (lane % HEAD_DIM) < HEAD_DIM // 2

    def rope(y):
        swapped = jnp.where(first_half, pltpu.roll(y, 96, 1), pltpu.roll(y, 32, 1))
        return y * rc + swapped * rs

    def project(lo_col, hi_col):
        return jnp.dot(hn, w_ref[:, lo_col:hi_col], preferred_element_type=F32)

    def slabs(z):
        return [z[:, LANES * j:LANES * (j + 1)] for j in range(z.shape[1] // LANES)]

    def qk_norm(z, g):
        wd = z.shape[1]
        ms = jnp.dot((z * z).astype(BF16), avg_ref[:wd, :wd], preferred_element_type=F32)
        y = z * lax.rsqrt(ms + EPS)
        return [rope(s * g) for s in slabs(y)]

    def one_per_slab(pair, fill):
        return [jnp.where(lo, pair, fill), jnp.where(lo, pltpu.roll(pair, HEAD_DIM, 1), fill)]

    def store(ref, parts):
        ref[...] = jnp.concatenate(parts, axis=1).astype(BF16)

    store(qa_ref, [s for pair in qk_norm(project(0, 512), gt[0:1]) for s in one_per_slab(pair, 0.0)])
    z = project(512, 768)
    store(ka_ref, one_per_slab(qk_norm(z[:, :LANES], gt[1:2])[0], 0.0))
    store(va_ref, one_per_slab(z[:, LANES:], 1.0))
    store(qb_ref, qk_norm(project(768, 1280), gt[2:3]))
    store(kb_ref, qk_norm(project(1280, 1792), gt[3:4]))
    ones = jnp.ones((TM, LANES), F32)
    store(vb_ref, [part for v in slabs(project(1792, 2304)) for part in (v, ones)])
    store(uc_ref, [project(2304, N_PROJ)])


def _proj_call(h, mods, g, w, gt, rc, rs, avg):
    widths = (1024, 256, 256, 512, 512, 1024, 512)
    return pl.pallas_call(
        _proj_kernel,
        grid=(NT_ALL,),
        in_specs=[pl.BlockSpec((TM, D_MODEL), lambda i: (i, 0)),
                  pl.BlockSpec((1, N_MOD, D_MODEL), lambda i: (_mod_set(i), 0, 0)),
                  _resident((1, D_MODEL)),
                  _resident((D_MODEL, N_PROJ)),
                  _resident((4, LANES)),
                  pl.BlockSpec((TM, LANES), lambda i: (jnp.where(i < NT_LAT, i % TILES_PER_BATCH, TILES_PER_BATCH), 0)),
                  pl.BlockSpec((TM, LANES), lambda i: (jnp.where(i < NT_LAT, i % TILES_PER_BATCH, TILES_PER_BATCH), 0)),
                  _resident((512, 512))],
        out_specs=[pl.BlockSpec((TM, wd), lambda i: (i, 0)) for wd in widths],
        out_shape=[jax.ShapeDtypeStruct((R_ALL, wd), BF16) for wd in widths],
        compiler_params=_cparams(1),
        name="proj",
    )(h, mods, g, w, gt, rc, rs, avg)


def _attn_pipeline(chunks, qs, k_of, v_of, s_ref, p_ref, al_ref, m_ref, acc_ref):
    width = lambda c: c[2]

    def qk(i):
        for st, q in enumerate(qs):
            s_ref[st, i % 2, :, :width(chunks[i])] = lax.dot_general(
                q, k_of(chunks[i], st), (((1,), (1,)), ((), ())), preferred_element_type=F32)

    def softmax(i):
        for st in range(len(qs)):
            s = s_ref[st, i % 2, :, :width(chunks[i])]
            m_old = m_ref[st]
            m_new = jnp.maximum(m_old, jnp.max(s, axis=-1, keepdims=True))
            al_ref[st, i % 2] = jnp.exp2(m_old - m_new)
            for j in range(width(chunks[i]) // LANES):
                cols = slice(LANES * j, LANES * (j + 1))
                p_ref[st, i % 2, :, cols] = jnp.exp2(s[:, cols] - m_new).astype(BF16)
            m_ref[st] = m_new

    def pv(i):
        for st in range(len(qs)):
            o = jnp.dot(p_ref[st, i % 2, :, :width(chunks[i])], v_of(chunks[i], st), preferred_element_type=F32)
            al = al_ref[st, i % 2]
            for j in range(acc_ref.shape[2] // LANES):
                cols = slice(LANES * j, LANES * (j + 1))
                acc_ref[st, :, cols] = al * acc_ref[st, :, cols] + o[:, cols]

    m_ref[...] = jnp.full_like(m_ref, -1e30)
    acc_ref[...] = jnp.zeros_like(acc_ref)
    n = len(chunks)
    qk(0)
    if n > 1:
        qk(1)
    softmax(0)
    for i in range(n - 2):
        pv(i)
        qk(i + 2)
        softmax(i + 1)
    if n > 1:
        pv(n - 2)
        softmax(n - 1)
    pv(n - 1)


def _chunk_list(ctx_queries):
    ctx = (True, 0, CTX_LEN)
    return [ctx] if ctx_queries else [(False, r, TK) for r in range(0, SEQ, TK)] + [ctx]


def _chunk_rows(c):
    return pl.ds(c[1], c[2])


def _split_attn_refs(refs, ctx_queries, n_extra):
    q_ref, kc_ref, vc_ref = refs[:3]
    kl_ref, vl_ref = (None, None) if ctx_queries else refs[3:5]
    rest = refs[3 if ctx_queries else 5:]
    return q_ref, kc_ref, vc_ref, kl_ref, vl_ref, rest[:n_extra], rest[n_extra], rest[n_extra + 1:]


def _attn_a_kernel(*refs, ctx_queries):
    q_ref, kc_ref, vc_ref, kl_ref, vl_ref, _, o_ref, scratch = _split_attn_refs(refs, ctx_queries, 0)
    tq = q_ref.shape[0]
    q4 = jnp.concatenate([q_ref[:, LANES * j:LANES * (j + 1)] for j in range(4)], axis=0)
    k_of = lambda c, st: (kc_ref if c[0] else kl_ref)[_chunk_rows(c), :]
    v_of = lambda c, st: (vc_ref if c[0] else vl_ref)[_chunk_rows(c), :]
    _attn_pipeline(_chunk_list(ctx_queries), [q4], k_of, v_of, *scratch)
    acc = scratch[-1][0]
    o = acc / pltpu.roll(acc, HEAD_DIM, 1)
    lo = lax.broadcasted_iota(jnp.int32, (tq, LANES), 1) < HEAD_DIM
    pairs = [jnp.where(lo, o[2 * j * tq:(2 * j + 1) * tq], pltpu.roll(o[(2 * j + 1) * tq:(2 * j + 2) * tq], HEAD_DIM, 1))
             for j in range(2)]
    o_ref[...] = jnp.concatenate(pairs, axis=1).astype(BF16)


def _attn_scratch(n_streams, rows, n_v):
    return [pltpu.VMEM((n_streams, 2, rows, TK), F32),
            pltpu.VMEM((n_streams, 2, rows, TK), BF16),
            pltpu.VMEM((n_streams, 2, rows, LANES), F32),
            pltpu.VMEM((n_streams, rows, LANES), F32),
            pltpu.VMEM((n_streams, rows, n_v), F32)]


def _attn_specs(ctx_queries, q_w, k_w, v_w):
    tq = CTX_LEN if ctx_queries else TQ
    ctx_blk = lambda b, g, qi: (CTX_BLK0 + b, g)
    q_blk = ctx_blk if ctx_queries else (lambda b, g, qi: (b * NQ + qi, g))
    o_blk = (lambda b, g, qi: (b, g)) if ctx_queries else q_blk
    specs = [pl.BlockSpec((tq, q_w), q_blk),
             pl.BlockSpec((CTX_LEN, k_w), ctx_blk),
             pl.BlockSpec((CTX_LEN, v_w), ctx_blk)]
    if not ctx_queries:
        lat_blk = lambda b, g, qi: (b, g)
        specs += [pl.BlockSpec((SEQ, k_w), lat_blk, pipeline_mode=pl.Buffered(1)),
                  pl.BlockSpec((SEQ, v_w), lat_blk, pipeline_mode=pl.Buffered(1))]
    return specs, pl.BlockSpec((tq, 256), o_blk)


def _attn_a_call(qa, ka, va, *, ctx_queries):
    in_specs, out_spec = _attn_specs(ctx_queries, 512, LANES, LANES)
    kv = (ka, va) if ctx_queries else (ka, va, ka, va)
    return pl.pallas_call(
        functools.partial(_attn_a_kernel, ctx_queries=ctx_queries),
        grid=(BATCH, 2, 1 if ctx_queries else NQ),
        in_specs=in_specs,
        out_specs=out_spec,
        out_shape=jax.ShapeDtypeStruct((N_CTX if ctx_queries else N_LAT, 512), BF16),
        scratch_shapes=_attn_scratch(1, 4 * (CTX_LEN if ctx_queries else TQ), LANES),
        compiler_params=_cparams(3),
        name="attn_a_ctx" if ctx_queries else "attn_a",
    )(qa, *kv)


def _attn_b_kernel(*refs, ctx_queries, lam_init):
    q_ref, kc_ref, vc_ref, kl_ref, vl_ref, (dl_ref, sg_ref), o_ref, scratch = _split_attn_refs(refs, ctx_queries, 2)
    tq = q_ref.shape[0]
    lo = lax.broadcasted_iota(jnp.int32, (tq, LANES), 1) < HEAD_DIM
    zero = jnp.zeros((tq, LANES), BF16)
    q2 = []
    for hh in range(2):
        qh = q_ref[:, LANES * hh:LANES * (hh + 1)]
        q2.append(jnp.concatenate([jnp.where(lo, qh, zero), jnp.where(lo, zero, qh)], axis=0))
    k_of = lambda c, hh: (kc_ref if c[0] else kl_ref)[_chunk_rows(c), LANES * hh:LANES * (hh + 1)]
    v_of = lambda c, hh: (vc_ref if c[0] else vl_ref)[_chunk_rows(c), 256 * hh:256 * (hh + 1)]
    _attn_pipeline(_chunk_list(ctx_queries), q2, k_of, v_of, *scratch)

    lp = dl_ref[...]
    lam = (jnp.exp(jnp.sum(lp[0:1] * lp[1:2], axis=-1, keepdims=True))
           - jnp.exp(jnp.sum(lp[2:3] * lp[3:4], axis=-1, keepdims=True)) + lam_init)
    outs = []
    for hh in range(2):
        acc = scratch[-1][hh]
        o = acc[:, :LANES] / acc[:, LANES:]
        d = o[:tq] - lam * o[tq:]
        y = d * lax.rsqrt(jnp.mean(d * d, axis=-1, keepdims=True) + EPS) * sg_ref[...]
        outs.append(y * (1.0 - lam_init))
    o_ref[...] = jnp.concatenate(outs, axis=1).astype(BF16)


def _attn_b_call(qb, kb, vb, diff_lam_l, subln_g, *, ctx_queries, lam_init):
    in_specs, out_spec = _attn_specs(ctx_queries, 256, 256, 512)
    in_specs += [pl.BlockSpec((4, HEAD_DIM), lambda b, hp, qi: (0, 0)),
                 pl.BlockSpec((1, LANES), lambda b, hp, qi: (0, 0))]
    kv = (kb, vb) if ctx_queries else (kb, vb, kb, vb)
    return pl.pallas_call(
        functools.partial(_attn_b_kernel, ctx_queries=ctx_queries, lam_init=lam_init),
        grid=(BATCH, 2, 1 if ctx_queries else NQ),
        in_specs=in_specs,
        out_specs=out_spec,
        out_shape=jax.ShapeDtypeStruct((N_CTX if ctx_queries else N_LAT, 512), BF16),
        scratch_shapes=_attn_scratch(2, 2 * (CTX_LEN if ctx_queries else TQ), 256),
        compiler_params=_cparams(3),
        name="attn_b_ctx" if ctx_queries else "attn_b",
    )(qb, *kv, diff_lam_l, subln_g)


def _fft1_kernel(x_ref, f_ref, tc_ref, ts_ref, yr_ref, yi_ref):
    y = jnp.dot(f_ref[...], x_ref[...], preferred_element_type=F32)
    for t in range(FFT_TN // 512):
        c = jnp.concatenate([tc_ref[:, LANES * t:LANES * (t + 1)]] * 4, axis=1)
        s = jnp.concatenate([ts_ref[:, LANES * t:LANES * (t + 1)]] * 4, axis=1)
        a = y[:FFT_N2, 512 * t:512 * (t + 1)]
        b = y[FFT_N2:, 512 * t:512 * (t + 1)]
        yr_ref[:, 512 * t:512 * (t + 1)] = (a * c + b * s).astype(BF16)
        yi_ref[:, 512 * t:512 * (t + 1)] = (b * c - a * s).astype(BF16)


def _fft1_call(uc_view, f_stack, tc, ts):
    n_col = FFT_N1 * 512
    return pl.pallas_call(
        _fft1_kernel,
        grid=(BATCH, n_col // FFT_TN),
        in_specs=[pl.BlockSpec((FFT_N2, FFT_TN), lambda b, j: (b, j)),
                  pl.BlockSpec((2 * FFT_N2, FFT_N2), lambda b, j: (0, 0)),
                  pl.BlockSpec((FFT_N2, FFT_TN // 4), lambda b, j: (0, j)),
                  pl.BlockSpec((FFT_N2, FFT_TN // 4), lambda b, j: (0, j))],
        out_specs=[pl.BlockSpec((FFT_N2, FFT_TN), lambda b, j: (b, j))] * 2,
        out_shape=[jax.ShapeDtypeStruct((BATCH * FFT_N2, n_col), BF16)] * 2,
        compiler_params=_cparams(2),
        name="fft1",
    )(uc_view, f_stack, tc, ts)


def _fft2_kernel(yr_ref, yi_ref, g_ref, zr_ref, zi_ref):
    for t in range(FFT_KB):
        ys = jnp.concatenate([yr_ref[t], yi_ref[t]], axis=0)
        z = jnp.dot(g_ref[...], ys, preferred_element_type=F32)
        zr_ref[:, 512 * t:512 * (t + 1)] = z[:FFT_N1].astype(BF16)
        zi_ref[:, 512 * t:512 * (t + 1)] = z[FFT_N1:].astype(BF16)


def _fft2_call(yr, yi, g_stack):
    steps = FFT_N2 // FFT_KB
    return pl.pallas_call(
        _fft2_kernel,
        grid=(BATCH, steps),
        in_specs=[pl.BlockSpec((FFT_KB, FFT_N1, 512), lambda b, j: (b * steps + j, 0, 0)),
                  pl.BlockSpec((FFT_KB, FFT_N1, 512), lambda b, j: (b * steps + j, 0, 0)),
                  pl.BlockSpec((2 * FFT_N1, 2 * FFT_N1), lambda b, j: (0, 0))],
        out_specs=[pl.BlockSpec((FFT_N1, FFT_KB * 512), lambda b, j: (b, j))] * 2,
        out_shape=[jax.ShapeDtypeStruct((BATCH * FFT_N1, FFT_N2 * 512), BF16)] * 2,
        compiler_params=_cparams(2),
        name="fft2",
    )(yr.reshape(BATCH * FFT_N2, FFT_N1, 512), yi.reshape(BATCH * FFT_N2, FFT_N1, 512), g_stack)


def _fft_ctx_kernel(x_ref, f_ref, zr_ref, zi_ref):
    z = jnp.dot(f_ref[...], x_ref[...], preferred_element_type=F32)
    zr_ref[...] = z[:CTX_LEN].astype(BF16)
    zi_ref[...] = z[CTX_LEN:].astype(BF16)


def _fft_ctx_call(uc, f_stack):
    return pl.pallas_call(
        _fft_ctx_kernel,
        grid=(BATCH,),
        in_specs=[pl.BlockSpec((CTX_LEN, 512), lambda b: (CTX_BLK0 + b, 0)),
                  pl.BlockSpec((2 * CTX_LEN, CTX_LEN), lambda b: (0, 0))],
        out_specs=[pl.BlockSpec((CTX_LEN, 512), lambda b: (b, 0))] * 2,
        out_shape=[jax.ShapeDtypeStruct((N_CTX, 512), BF16)] * 2,
        compiler_params=_cparams(1),
        name="fft_ctx",
    )(uc, f_stack)


def _merge_kernel(h_ref, mods_ref, g_ref, *rest, with_ctx):
    lat = [r[...] for r in rest[:4]]
    if with_ctx:
        is_ctx = pl.program_id(0) == NT_LAT
        ya, yb, zr, zi = [jnp.where(is_ctx, c[...], x) for c, x in zip(rest[4:8], lat)]
        wg_ref, wb_ref, wo_ref, cs_ref, o_ref = rest[8:]
    else:
        ya, yb, zr, zi = lat
        wg_ref, wb_ref, wo_ref, cs_ref, o_ref = rest[4:]
    h = h_ref[...]
    m = mods_ref[0]
    hn = _rms_mod(h, g_ref[...], m[3:4], m[4:5]).astype(BF16)
    yc = jnp.concatenate(
        [jnp.dot(jnp.concatenate([zr[:, LANES * g:LANES * (g + 1)], zi[:, LANES * g:LANES * (g + 1)]], axis=1),
                 cs_ref[...], preferred_element_type=F32) for g in range(4)], axis=1).astype(BF16)
    branches = (ya, yb, yc)
    mix = None
    for i in range(3):
        gl = jnp.dot(hn, wg_ref[:, D_MODEL * i:D_MODEL * (i + 1)], preferred_element_type=F32)
        pr = jnp.dot(branches[i], wb_ref[i], preferred_element_type=F32)
        t = jax.nn.sigmoid(gl) * pr
        mix = t if mix is None else mix + t
    o_ref[...] = h + m[5:6] * jnp.dot(mix.astype(BF16), wo_ref[...], preferred_element_type=F32)


def _merge_call(h, mods, g, lat, ctx, wg, wb, wo, cs):
    with_ctx = ctx is not None
    n_tiles = NT_ALL if with_ctx else NT_LAT
    row = lambda i: (i, 0)
    lat_row = lambda i: (jnp.minimum(i, NT_LAT - 1), 0)
    ctx_specs = [_resident((N_CTX, 512))] * 4 if with_ctx else []
    return pl.pallas_call(
        functools.partial(_merge_kernel, with_ctx=with_ctx),
        grid=(n_tiles,),
        in_specs=[pl.BlockSpec((TM, D_MODEL), row),
                  pl.BlockSpec((1, N_MOD, D_MODEL), lambda i: (_mod_set(i), 0, 0)),
                  _resident((1, D_MODEL))]
                 + [pl.BlockSpec((TM, 512), lat_row)] * 4
                 + ctx_specs
                 + [_resident((D_MODEL, 3 * D_MODEL)),
                    _resident((3, 512, D_MODEL)),
                    _resident((D_MODEL, D_MODEL)),
                    _resident((2 * LANES, LANES))],
        out_specs=pl.BlockSpec((TM, D_MODEL), row),
        out_shape=jax.ShapeDtypeStruct((n_tiles * TM, D_MODEL), F32),
        compiler_params=_cparams(1),
        name="merge",
    )(h, mods, g, *lat, *(ctx or ()), wg, wb, wo, cs)


def _dft_cos_sin(n, rows=None, cols=None):
    j = jnp.arange(n if rows is None else rows, dtype=jnp.int32)[:, None]
    k = jnp.arange(n if cols is None else cols, dtype=jnp.int32)[None, :]
    ang = ((j * k) % n).astype(F32) * (2.0 * math.pi / n)
    return jnp.cos(ang), jnp.sin(ang)


def _rope_tables():
    rows = SEQ // GRID_W
    row = jnp.repeat(jnp.arange(rows, dtype=F32), GRID_W)
    col = jnp.tile(jnp.arange(GRID_W, dtype=F32), rows)
    n_freq = HEAD_DIM // 4
    inv = jnp.power(ROPE_THETA, -jnp.arange(n_freq, dtype=F32) / n_freq)
    ang = jnp.concatenate([row[:, None] * inv, col[:, None] * inv], axis=-1)
    cos, sin = jnp.cos(ang), jnp.sin(ang)
    rc = jnp.tile(cos, (1, 4))
    rs = jnp.tile(jnp.concatenate([-sin, sin], axis=-1), (1, 2))
    rc = jnp.concatenate([rc, jnp.ones((TM, LANES), F32)], axis=0)
    rs = jnp.concatenate([rs, jnp.zeros((TM, LANES), F32)], axis=0)
    return rc, rs


def kernel(x, c, ctx, c_ctx, w_ada, b_ada, norm_g, ffn_wi, ffn_wo, w_in, qk_g, diff_lam, diff_subln_g, w_branch, w_out):
    rc, rs = _rope_tables()
    c128, s128 = _dft_cos_sin(FFT_N2)
    f1 = (jnp.concatenate([c128, -s128], axis=0) * FFT_N2 ** -0.5).astype(BF16)
    c64, s64 = _dft_cos_sin(FFT_N1)
    g2 = (jnp.concatenate([jnp.concatenate([c64, s64], axis=1),
                           jnp.concatenate([-s64, c64], axis=1)], axis=0) * FFT_N1 ** -0.5).astype(BF16)
    twc, tws = _dft_cos_sin(SEQ, rows=FFT_N2, cols=FFT_N1)
    twc = jnp.repeat(twc, LANES, axis=1)
    tws = jnp.repeat(tws, LANES, axis=1)
    cctx, sctx = _dft_cos_sin(CTX_LEN)
    fctx = (jnp.concatenate([cctx, -sctx], axis=0) * CTX_LEN ** -0.5).astype(BF16)
    cs = (jnp.concatenate([c128, s128], axis=0) * LANES ** -0.5).astype(BF16)
    avg = jnp.kron(jnp.eye(8, dtype=F32), jnp.full((HEAD_DIM, HEAD_DIM), 1.0 / HEAD_DIM, F32)).astype(BF16)

    ct = jnp.zeros((D_MODEL, 8), F32).at[:, 0:BATCH].set(c.T).at[:, BATCH].set(c_ctx)
    mods_all = _ada_call(ct, w_ada, b_ada).reshape(DEPTH, 8, N_MOD, D_MODEL)[:, :3]

    h = x.reshape(N_LAT, D_MODEL)
    q_mul = HEAD_DIM ** -0.5 * math.log2(math.e)
    q_scale = jnp.array([q_mul, 1.0, q_mul, 1.0], F32)[:, None]
    for l in range(DEPTH):
        last = l == DEPTH - 1
        mods = mods_all[l]
        ng = norm_g[l].reshape(3, 1, D_MODEL)
        wi = [ffn_wi[l, k].astype(BF16) for k in range(2)]
        wo = [ffn_wo[l, k].astype(BF16) for k in range(2)]
        w_proj = w_in[l][:, :N_PROJ].astype(BF16)
        w_gate = w_in[l][:, N_PROJ:].astype(BF16)
        gt = jnp.tile(qk_g[l], (1, 2)) * q_scale
        lam_init = 0.8 - 0.6 * math.exp(-0.3 * l)
        subln = diff_subln_g[l][None, :]

        h = _ffn_call(h, mods, ng[0], wi[0], wo[0], k=0, n_tiles=NT_ALL,
                      h_ctx=ctx.reshape(N_CTX, D_MODEL) if l == 0 else None)
        qa, ka, va, qb, kb, vb, uc = _proj_call(h, mods, ng[1], w_proj, gt, rc, rs, avg)
        ya = _attn_a_call(qa, ka, va, ctx_queries=False)
        yb = _attn_b_call(qb, kb, vb, diff_lam[l], subln, ctx_queries=False, lam_init=lam_init)
        yr, yi = _fft1_call(uc.reshape(R_ALL * 512 // (FFT_N1 * 512), FFT_N1 * 512), f1, twc, tws)
        zr, zi = _fft2_call(yr, yi, g2)
        lat = (ya, yb, zr.reshape(N_LAT, 512), zi.reshape(N_LAT, 512))
        ctx_parts = None
        if not last:
            ctx_parts = (_attn_a_call(qa, ka, va, ctx_queries=True),
                         _attn_b_call(qb, kb, vb, diff_lam[l], subln, ctx_queries=True, lam_init=lam_init),
                         *_fft_ctx_call(uc, fctx))
        h = _merge_call(h, mods, ng[1], lat, ctx_parts, w_gate, w_branch[l].astype(BF16), w_out[l].astype(BF16), cs)
        n_tiles = NT_LAT if last else NT_ALL
        h = _ffn_call(h, mods, ng[2], wi[1], wo[1], k=2, n_tiles=n_tiles)
    return h.reshape(BATCH, SEQ, D_MODEL)
```

```python
import functools
import math

import jax
import jax.numpy as jnp
from jax import lax
from jax.experimental import pallas as pl
from jax.experimental.pallas import tpu as pltpu

F32 = jnp.float32
BF16 = jnp.bfloat16

D_MODEL = 1024
BATCH = 2
SEQ = 8192
DEPTH = 2
GRID_W = 64
CTX_LEN = 256
HEAD_DIM = 64
D_FF = 2816
N_MOD = 9
ROPE_THETA = 10000.0
EPS = 1e-6

LANES = 128
N_LAT = BATCH * SEQ
N_CTX = BATCH * CTX_LEN
R_ALL = N_LAT + N_CTX
TM = 512
NT_LAT = N_LAT // TM
NT_ALL = R_ALL // TM
TILES_PER_BATCH = SEQ // TM
FF_CHUNK = 256
TQ_A = 512
TQ_B = 256
CTX_BLK0 = N_LAT // CTX_LEN
TK = 1024
ADA_TN = 1152
FFT_N1 = 64
FFT_N2 = 128
FFT_TN = 4096
FFT_KB = 16
VMEM_LIMIT = 56 * 1024 * 1024


def _cparams(n_axes):
    return pltpu.CompilerParams(dimension_semantics=("arbitrary",) * n_axes,
                                vmem_limit_bytes=VMEM_LIMIT)


def _resident(shape):
    nd = len(shape)
    return pl.BlockSpec(shape, lambda *_: (0,) * nd, pipeline_mode=pl.Buffered(1))


def _mod_set(i):
    return jnp.minimum(i // TILES_PER_BATCH, 2)


def _rms_mod(h, g, shift, scale):
    y = h * lax.rsqrt(jnp.mean(h * h, axis=-1, keepdims=True) + EPS) * g
    return y * (1.0 + scale) + shift


def _ada_kernel(ct_ref, w_ref, b_ref, o_ref):
    ct = ct_ref[...]
    s = ct * jax.nn.sigmoid(ct)
    w = w_ref[0]
    rows = [jnp.sum(w * s[:, r:r + 1], axis=0, keepdims=True) for r in range(3)]
    rows.append(jnp.zeros((5, ADA_TN), F32))
    o_ref[0] = jnp.concatenate(rows, axis=0) + b_ref[0]


def _ada_call(ct, w_ada, b_ada):
    n_col = N_MOD * D_MODEL
    return pl.pallas_call(
        _ada_kernel,
        grid=(DEPTH, n_col // ADA_TN),
        in_specs=[pl.BlockSpec((D_MODEL, 8), lambda l, j: (0, 0)),
                  pl.BlockSpec((1, D_MODEL, ADA_TN), lambda l, j: (l, 0, j)),
                  pl.BlockSpec((1, 1, ADA_TN), lambda l, j: (l, 0, j))],
        out_specs=pl.BlockSpec((1, 8, ADA_TN), lambda l, j: (l, 0, j)),
        out_shape=jax.ShapeDtypeStruct((DEPTH, 8, n_col), F32),
        compiler_params=_cparams(2),
        name="ada",
    )(ct, w_ada, b_ada.reshape(DEPTH, 1, n_col))


def _ffn_kernel(*refs, k, split_rows):
    if split_rows:
        h_ref, hc_ref, mods_ref, g_ref, wi_ref, wo_ref, o_ref = refs
        h = jnp.where(pl.program_id(0) == NT_LAT, hc_ref[...], h_ref[...])
    else:
        h_ref, mods_ref, g_ref, wi_ref, wo_ref, o_ref = refs
        h = h_ref[...]
    m = mods_ref[0]
    hn = _rms_mod(h, g_ref[...], m[3 * k:3 * k + 1], m[3 * k + 1:3 * k + 2]).astype(BF16)
    acc = None
    for lo in range(0, D_FF, FF_CHUNK):
        cols = slice(lo, min(lo + FF_CHUNK, D_FF))
        g = jnp.dot(hn, wi_ref[:, cols], preferred_element_type=F32)
        u = jnp.dot(hn, wi_ref[:, D_FF + cols.start:D_FF + cols.stop], preferred_element_type=F32)
        a = (g * jax.nn.sigmoid(g) * u).astype(BF16)
        o = jnp.dot(a, wo_ref[cols, :], preferred_element_type=F32)
        acc = o if acc is None else acc + o
    o_ref[...] = h + 0.5 * m[3 * k + 2:3 * k + 3] * acc


def _ffn_call(h, mods, g, wi, wo, *, k, n_tiles, h_ctx=None):
    rows = n_tiles * TM
    split_rows = h_ctx is not None
    if split_rows:
        row_specs = [pl.BlockSpec((TM, D_MODEL), lambda i: (jnp.minimum(i, NT_LAT - 1), 0)),
                     _resident((N_CTX, D_MODEL))]
    else:
        row_specs = [pl.BlockSpec((TM, D_MODEL), lambda i: (i, 0))]
    return pl.pallas_call(
        functools.partial(_ffn_kernel, k=k, split_rows=split_rows),
        grid=(n_tiles,),
        in_specs=row_specs
                 + [pl.BlockSpec((1, N_MOD, D_MODEL), lambda i: (_mod_set(i), 0, 0)),
                    _resident((1, D_MODEL)),
                    _resident((D_MODEL, 2 * D_FF)),
                    _resident((D_FF, D_MODEL))],
        out_specs=pl.BlockSpec((TM, D_MODEL), lambda i: (i, 0)),
        out_shape=jax.ShapeDtypeStruct((rows, D_MODEL), F32),
        compiler_params=_cparams(1),
        name=f"ffn{k}",
    )(h, *((h_ctx,) if split_rows else ()), mods, g, wi, wo)


N_PROJ = 2816


def _proj_kernel(h_ref, mods_ref, g_ref, w_ref, gt_ref, rc_ref, rs_ref, avg_ref,
                 qa_ref, ka_ref, va_ref, qb_ref, kb_ref, vb_ref, uc_ref):
    h = h_ref[...]
    m = mods_ref[0]
    hn = _rms_mod(h, g_ref[...], m[3:4], m[4:5]).astype(BF16)
    rc = rc_ref[...]
    rs = rs_ref[...]
    gt = gt_ref[...]
    lane = lax.broadcasted_iota(jnp.int32, (TM, LANES), 1)
    lo = lane < HEAD_DIM
    first_half = (lane % HEAD_DIM) < HEAD_DIM // 2

    def rope(y):
        swapped = jnp.where(first_half, pltpu.roll(y, 96, 1), pltpu.roll(y, 32, 1))
        return y * rc + swapped * rs

    def project(lo_col, hi_col):
        return jnp.dot(hn, w_ref[:, lo_col:hi_col], preferred_element_type=F32)

    def slabs(z):
        return [z[:, LANES * j:LANES * (j + 1)] for j in range(z.shape[1] // LANES)]

    def qk_norm(z, g):
        wd = z.shape[1]
        ms = jnp.dot((z * z).astype(BF16), avg_ref[:wd, :wd], preferred_element_type=F32)
        y = z * lax.rsqrt(ms + EPS)
        return [rope(s * g) for s in slabs(y)]

    def one_per_slab(pair, fill):
        return [jnp.where(lo, pair, fill), jnp.where(lo, pltpu.roll(pair, HEAD_DIM, 1), fill)]

    def store(ref, parts):
        ref[...] = jnp.concatenate(parts, axis=1).astype(BF16)

    store(qa_ref, [s for pair in qk_norm(project(0, 512), gt[0:1]) for s in one_per_slab(pair, 0.0)])
    z = project(512, 768)
    store(ka_ref, one_per_slab(qk_norm(z[:, :LANES], gt[1:2])[0], 0.0))
    store(va_ref, one_per_slab(z[:, LANES:], 1.0))
    store(qb_ref, qk_norm(project(768, 1280), gt[2:3]))
    store(kb_ref, qk_norm(project(1280, 1792), gt[3:4]))
    ones = jnp.ones((TM, LANES), F32)
    store(vb_ref, [part for v in slabs(project(1792, 2304)) for part in (v, ones)])
    store(uc_ref, [project(2304, N_PROJ)])


def _proj_call(h, mods, g, w, gt, rc, rs, avg):
    widths = (1024, 256, 256, 512, 512, 1024, 512)
    return pl.pallas_call(
        _proj_kernel,
        grid=(NT_ALL,),
        in_specs=[pl.BlockSpec((TM, D_MODEL), lambda i: (i, 0)),
                  pl.BlockSpec((1, N_MOD, D_MODEL), lambda i: (_mod_set(i), 0, 0)),
                  _resident((1, D_MODEL)),
                  _resident((D_MODEL, N_PROJ)),
                  _resident((4, LANES)),
                  pl.BlockSpec((TM, LANES), lambda i: (jnp.where(i < NT_LAT, i % TILES_PER_BATCH, TILES_PER_BATCH), 0)),
                  pl.BlockSpec((TM, LANES), lambda i: (jnp.where(i < NT_LAT, i % TILES_PER_BATCH, TILES_PER_BATCH), 0)),
                  _resident((512, 512))],
        out_specs=[pl.BlockSpec((TM, wd), lambda i: (i, 0)) for wd in widths],
        out_shape=[jax.ShapeDtypeStruct((R_ALL, wd), BF16) for wd in widths],
        compiler_params=_cparams(1),
        name="proj",
    )(h, mods, g, w, gt, rc, rs, avg)


def _attn_pipeline(chunks, qs, k_of, v_of, s_ref, p_ref, al_ref, m_ref, acc_ref):
    width = lambda c: c[2]

    def qk(i):
        for st, q in enumerate(qs):
            s_ref[st, i % 2, :, :width(chunks[i])] = lax.dot_general(
                q, k_of(chunks[i], st), (((1,), (1,)), ((), ())), preferred_element_type=F32)

    def softmax(i):
        for st in range(len(qs)):
            s = s_ref[st, i % 2, :, :width(chunks[i])]
            m_old = m_ref[st]
            m_new = jnp.maximum(m_old, jnp.max(s, axis=-1, keepdims=True))
            al_ref[st, i % 2] = jnp.exp2(m_old - m_new)
            for j in range(width(chunks[i]) // LANES):
                cols = slice(LANES * j, LANES * (j + 1))
                p_ref[st, i % 2, :, cols] = jnp.exp2(s[:, cols] - m_new).astype(BF16)
            m_ref[st] = m_new

    def pv(i):
        for st in range(len(qs)):
            o = jnp.dot(p_ref[st, i % 2, :, :width(chunks[i])], v_of(chunks[i], st), preferred_element_type=F32)
            al = al_ref[st, i % 2]
            for j in range(acc_ref.shape[2] // LANES):
                cols = slice(LANES * j, LANES * (j + 1))
                acc_ref[st, :, cols] = al * acc_ref[st, :, cols] + o[:, cols]

    m_ref[...] = jnp.full_like(m_ref, -1e30)
    acc_ref[...] = jnp.zeros_like(acc_ref)
    n = len(chunks)
    qk(0)
    if n > 1:
        qk(1)
    softmax(0)
    for i in range(n - 2):
        pv(i)
        qk(i + 2)
        softmax(i + 1)
    if n > 1:
        pv(n - 2)
        softmax(n - 1)
    pv(n - 1)


def _chunk_list(ctx_queries):
    ctx = (True, 0, CTX_LEN)
    return [ctx] if ctx_queries else [(False, r, TK) for r in range(0, SEQ, TK)] + [ctx]


def _chunk_rows(c):
    return pl.ds(c[1], c[2])


def _split_attn_refs(refs, ctx_queries, n_extra):
    q_ref, kc_ref, vc_ref = refs[:3]
    kl_ref, vl_ref = (None, None) if ctx_queries else refs[3:5]
    rest = refs[3 if ctx_queries else 5:]
    return q_ref, kc_ref, vc_ref, kl_ref, vl_ref, rest[:n_extra], rest[n_extra], rest[n_extra + 1:]


def _attn_a_kernel(*refs, ctx_queries):
    q_ref, kc_ref, vc_ref, kl_ref, vl_ref, _, o_ref, scratch = _split_attn_refs(refs, ctx_queries, 0)
    tq = q_ref.shape[0]
    q4 = jnp.concatenate([q_ref[:, LANES * j:LANES * (j + 1)] for j in range(4)], axis=0)
    k_of = lambda c, st: (kc_ref if c[0] else kl_ref)[_chunk_rows(c), :]
    v_of = lambda c, st: (vc_ref if c[0] else vl_ref)[_chunk_rows(c), :]
    _attn_pipeline(_chunk_list(ctx_queries), [q4], k_of, v_of, *scratch)
    acc = scratch[-1][0]
    o = acc / pltpu.roll(acc, HEAD_DIM, 1)
    lo = lax.broadcasted_iota(jnp.int32, (tq, LANES), 1) < HEAD_DIM
    pairs = [jnp.where(lo, o[2 * j * tq:(2 * j + 1) * tq], pltpu.roll(o[(2 * j + 1) * tq:(2 * j + 2) * tq], HEAD_DIM, 1))
             for j in range(2)]
    o_ref[...] = jnp.concatenate(pairs, axis=1).astype(BF16)


def _attn_scratch(n_streams, rows, n_v):
    return [pltpu.VMEM((n_streams, 2, rows, TK), F32),
            pltpu.VMEM((n_streams, 2, rows, TK), BF16),
            pltpu.VMEM((n_streams, 2, rows, LANES), F32),
            pltpu.VMEM((n_streams, rows, LANES), F32),
            pltpu.VMEM((n_streams, rows, n_v), F32)]


def _attn_specs(ctx_queries, tq, q_w, k_w, v_w):
    tq = CTX_LEN if ctx_queries else tq
    nq = SEQ // tq
    ctx_blk = lambda b, g, qi: (CTX_BLK0 + b, g)
    q_blk = ctx_blk if ctx_queries else (lambda b, g, qi: (b * nq + qi, g))
    o_blk = (lambda b, g, qi: (b, g)) if ctx_queries else q_blk
    specs = [pl.BlockSpec((tq, q_w), q_blk),
             pl.BlockSpec((CTX_LEN, k_w), ctx_blk),
             pl.BlockSpec((CTX_LEN, v_w), ctx_blk)]
    if not ctx_queries:
        lat_blk = lambda b, g, qi: (b, g)
        specs += [pl.BlockSpec((SEQ, k_w), lat_blk, pipeline_mode=pl.Buffered(1)),
                  pl.BlockSpec((SEQ, v_w), lat_blk, pipeline_mode=pl.Buffered(1))]
    grid = (BATCH, 2, 1 if ctx_queries else nq)
    return grid, specs, pl.BlockSpec((tq, 256), o_blk), tq


def _attn_a_call(qa, ka, va, *, ctx_queries):
    grid, in_specs, out_spec, tq = _attn_specs(ctx_queries, TQ_A, 512, LANES, LANES)
    kv = (ka, va) if ctx_queries else (ka, va, ka, va)
    return pl.pallas_call(
        functools.partial(_attn_a_kernel, ctx_queries=ctx_queries),
        grid=grid,
        in_specs=in_specs,
        out_specs=out_spec,
        out_shape=jax.ShapeDtypeStruct((N_CTX if ctx_queries else N_LAT, 512), BF16),
        scratch_shapes=_attn_scratch(1, 4 * tq, LANES),
        compiler_params=_cparams(3),
        name="attn_a_ctx" if ctx_queries else "attn_a",
    )(qa, *kv)


def _attn_b_kernel(*refs, ctx_queries, lam_init):
    q_ref, kc_ref, vc_ref, kl_ref, vl_ref, (dl_ref, sg_ref), o_ref, scratch = _split_attn_refs(refs, ctx_queries, 2)
    tq = q_ref.shape[0]
    lo = lax.broadcasted_iota(jnp.int32, (tq, LANES), 1) < HEAD_DIM
    zero = jnp.zeros((tq, LANES), BF16)
    q2 = []
    for hh in range(2):
        qh = q_ref[:, LANES * hh:LANES * (hh + 1)]
        q2.append(jnp.concatenate([jnp.where(lo, qh, zero), jnp.where(lo, zero, qh)], axis=0))
    k_of = lambda c, hh: (kc_ref if c[0] else kl_ref)[_chunk_rows(c), LANES * hh:LANES * (hh + 1)]
    v_of = lambda c, hh: (vc_ref if c[0] else vl_ref)[_chunk_rows(c), 256 * hh:256 * (hh + 1)]
    _attn_pipeline(_chunk_list(ctx_queries), q2, k_of, v_of, *scratch)

    lp = dl_ref[...]
    lam = (jnp.exp(jnp.sum(lp[0:1] * lp[1:2], axis=-1, keepdims=True))
           - jnp.exp(jnp.sum(lp[2:3] * lp[3:4], axis=-1, keepdims=True)) + lam_init)
    outs = []
    for hh in range(2):
        acc = scratch[-1][hh]
        o = acc[:, :LANES] / acc[:, LANES:]
        d = o[:tq] - lam * o[tq:]
        y = d * lax.rsqrt(jnp.mean(d * d, axis=-1, keepdims=True) + EPS) * sg_ref[...]
        outs.append(y * (1.0 - lam_init))
    o_ref[...] = jnp.concatenate(outs, axis=1).astype(BF16)


def _attn_b_call(qb, kb, vb, diff_lam_l, subln_g, *, ctx_queries, lam_init):
    grid, in_specs, out_spec, tq = _attn_specs(ctx_queries, TQ_B, 256, 256, 512)
    in_specs += [pl.BlockSpec((4, HEAD_DIM), lambda b, hp, qi: (0, 0)),
                 pl.BlockSpec((1, LANES), lambda b, hp, qi: (0, 0))]
    kv = (kb, vb) if ctx_queries else (kb, vb, kb, vb)
    return pl.pallas_call(
        functools.partial(_attn_b_kernel, ctx_queries=ctx_queries, lam_init=lam_init),
        grid=grid,
        in_specs=in_specs,
        out_specs=out_spec,
        out_shape=jax.ShapeDtypeStruct((N_CTX if ctx_queries else N_LAT, 512), BF16),
        scratch_shapes=_attn_scratch(2, 2 * tq, 256),
        compiler_params=_cparams(3),
        name="attn_b_ctx" if ctx_queries else "attn_b",
    )(qb, *kv, diff_lam_l, subln_g)


def _fft1_kernel(x_ref, f_ref, tc_ref, ts_ref, yr_ref, yi_ref):
    y = jnp.dot(f_ref[...], x_ref[...], preferred_element_type=F32)
    for t in range(FFT_TN // 512):
        c = jnp.concatenate([tc_ref[:, LANES * t:LANES * (t + 1)]] * 4, axis=1)
        s = jnp.concatenate([ts_ref[:, LANES * t:LANES * (t + 1)]] * 4, axis=1)
        a = y[:FFT_N2, 512 * t:512 * (t + 1)]
        b = y[FFT_N2:, 512 * t:512 * (t + 1)]
        yr_ref[:, 512 * t:512 * (t + 1)] = (a * c + b * s).astype(BF16)
        yi_ref[:, 512 * t:512 * (t + 1)] = (b * c - a * s).astype(BF16)


def _fft1_call(uc_view, f_stack, tc, ts):
    n_col = FFT_N1 * 512
    return pl.pallas_call(
        _fft1_kernel,
        grid=(BATCH, n_col // FFT_TN),
        in_specs=[pl.BlockSpec((FFT_N2, FFT_TN), lambda b, j: (b, j)),
                  pl.BlockSpec((2 * FFT_N2, FFT_N2), lambda b, j: (0, 0)),
                  pl.BlockSpec((FFT_N2, FFT_TN // 4), lambda b, j: (0, j)),
                  pl.BlockSpec((FFT_N2, FFT_TN // 4), lambda b, j: (0, j))],
        out_specs=[pl.BlockSpec((FFT_N2, FFT_TN), lambda b, j: (b, j))] * 2,
        out_shape=[jax.ShapeDtypeStruct((BATCH * FFT_N2, n_col), BF16)] * 2,
        compiler_params=_cparams(2),
        name="fft1",
    )(uc_view, f_stack, tc, ts)


def _fft2_kernel(yr_ref, yi_ref, g_ref, zr_ref, zi_ref):
    for t in range(FFT_KB):
        ys = jnp.concatenate([yr_ref[t], yi_ref[t]], axis=0)
        z = jnp.dot(g_ref[...], ys, preferred_element_type=F32)
        zr_ref[:, 512 * t:512 * (t + 1)] = z[:FFT_N1].astype(BF16)
        zi_ref[:, 512 * t:512 * (t + 1)] = z[FFT_N1:].astype(BF16)


def _fft2_call(yr, yi, g_stack):
    steps = FFT_N2 // FFT_KB
    return pl.pallas_call(
        _fft2_kernel,
        grid=(BATCH, steps),
        in_specs=[pl.BlockSpec((FFT_KB, FFT_N1, 512), lambda b, j: (b * steps + j, 0, 0)),
                  pl.BlockSpec((FFT_KB, FFT_N1, 512), lambda b, j: (b * steps + j, 0, 0)),
                  pl.BlockSpec((2 * FFT_N1, 2 * FFT_N1), lambda b, j: (0, 0))],
        out_specs=[pl.BlockSpec((FFT_N1, FFT_KB * 512), lambda b, j: (b, j))] * 2,
        out_shape=[jax.ShapeDtypeStruct((BATCH * FFT_N1, FFT_N2 * 512), BF16)] * 2,
        compiler_params=_cparams(2),
        name="fft2",
    )(yr.reshape(BATCH * FFT_N2, FFT_N1, 512), yi.reshape(BATCH * FFT_N2, FFT_N1, 512), g_stack)


def _fft_ctx_kernel(x_ref, f_ref, zr_ref, zi_ref):
    z = jnp.dot(f_ref[...], x_ref[...], preferred_element_type=F32)
    zr_ref[...] = z[:CTX_LEN].astype(BF16)
    zi_ref[...] = z[CTX_LEN:].astype(BF16)


def _fft_ctx_call(uc, f_stack):
    return pl.pallas_call(
        _fft_ctx_kernel,
        grid=(BATCH,),
        in_specs=[pl.BlockSpec((CTX_LEN, 512), lambda b: (CTX_BLK0 + b, 0)),
                  pl.BlockSpec((2 * CTX_LEN, CTX_LEN), lambda b: (0, 0))],
        out_specs=[pl.BlockSpec((CTX_LEN, 512), lambda b: (b, 0))] * 2,
        out_shape=[jax.ShapeDtypeStruct((N_CTX, 512), BF16)] * 2,
        compiler_params=_cparams(1),
        name="fft_ctx",
    )(uc, f_stack)


def _merge_kernel(h_ref, mods_ref, g_ref, *rest, with_ctx):
    lat = [r[...] for r in rest[:4]]
    if with_ctx:
        is_ctx = pl.program_id(0) == NT_LAT
        ya, yb, zr, zi = [jnp.where(is_ctx, c[...], x) for c, x in zip(rest[4:8], lat)]
        wg_ref, wb_ref, wo_ref, cs_ref, o_ref = rest[8:]
    else:
        ya, yb, zr, zi = lat
        wg_ref, wb_ref, wo_ref, cs_ref, o_ref = rest[4:]
    h = h_ref[...]
    m = mods_ref[0]
    hn = _rms_mod(h, g_ref[...], m[3:4], m[4:5]).astype(BF16)
    yc = jnp.concatenate(
        [jnp.dot(jnp.concatenate([zr[:, LANES * g:LANES * (g + 1)], zi[:, LANES * g:LANES * (g + 1)]], axis=1),
                 cs_ref[...], preferred_element_type=F32) for g in range(4)], axis=1).astype(BF16)
    branches = (ya, yb, yc)
    mix = None
    for i in range(3):
        gl = jnp.dot(hn, wg_ref[:, D_MODEL * i:D_MODEL * (i + 1)], preferred_element_type=F32)
        pr = jnp.dot(branches[i], wb_ref[i], preferred_element_type=F32)
        t = jax.nn.sigmoid(gl) * pr
        mix = t if mix is None else mix + t
    o_ref[...] = h + m[5:6] * jnp.dot(mix.astype(BF16), wo_ref[...], preferred_element_type=F32)


def _merge_call(h, mods, g, lat, ctx, wg, wb, wo, cs):
    with_ctx = ctx is not None
    n_tiles = NT_ALL if with_ctx else NT_LAT
    row = lambda i: (i, 0)
    lat_row = lambda i: (jnp.minimum(i, NT_LAT - 1), 0)
    ctx_specs = [_resident((N_CTX, 512))] * 4 if with_ctx else []
    return pl.pallas_call(
        functools.partial(_merge_kernel, with_ctx=with_ctx),
        grid=(n_tiles,),
        in_specs=[pl.BlockSpec((TM, D_MODEL), row),
                  pl.BlockSpec((1, N_MOD, D_MODEL), lambda i: (_mod_set(i), 0, 0)),
                  _resident((1, D_MODEL))]
                 + [pl.BlockSpec((TM, 512), lat_row)] * 4
                 + ctx_specs
                 + [_resident((D_MODEL, 3 * D_MODEL)),
                    _resident((3, 512, D_MODEL)),
                    _resident((D_MODEL, D_MODEL)),
                    _resident((2 * LANES, LANES))],
        out_specs=pl.BlockSpec((TM, D_MODEL), row),
        out_shape=jax.ShapeDtypeStruct((n_tiles * TM, D_MODEL), F32),
        compiler_params=_cparams(1),
        name="merge",
    )(h, mods, g, *lat, *(ctx or ()), wg, wb, wo, cs)


def _dft_cos_sin(n, rows=None, cols=None):
    j = jnp.arange(n if rows is None else rows, dtype=jnp.int32)[:, None]
    k = jnp.arange(n if cols is None else cols, dtype=jnp.int32)[None, :]
    ang = ((j * k) % n).astype(F32) * (2.0 * math.pi / n)
    return jnp.cos(ang), jnp.sin(ang)


def _rope_tables():
    rows = SEQ // GRID_W
    row = jnp.repeat(jnp.arange(rows, dtype=F32), GRID_W)
    col = jnp.tile(jnp.arange(GRID_W, dtype=F32), rows)
    n_freq = HEAD_DIM // 4
    inv = jnp.power(ROPE_THETA, -jnp.arange(n_freq, dtype=F32) / n_freq)
    ang = jnp.concatenate([row[:, None] * inv, col[:, None] * inv], axis=-1)
    cos, sin = jnp.cos(ang), jnp.sin(ang)
    rc = jnp.tile(cos, (1, 4))
    rs = jnp.tile(jnp.concatenate([-sin, sin], axis=-1), (1, 2))
    rc = jnp.concatenate([rc, jnp.ones((TM, LANES), F32)], axis=0)
    rs = jnp.concatenate([rs, jnp.zeros((TM, LANES), F32)], axis=0)
    return rc, rs


def kernel(x, c, ctx, c_ctx, w_ada, b_ada, norm_g, ffn_wi, ffn_wo, w_in, qk_g, diff_lam, diff_subln_g, w_branch, w_out):
    rc, rs = _rope_tables()
    c128, s128 = _dft_cos_sin(FFT_N2)
    f1 = (jnp.concatenate([c128, -s128], axis=0) * FFT_N2 ** -0.5).astype(BF16)
    c64, s64 = _dft_cos_sin(FFT_N1)
    g2 = (jnp.concatenate([jnp.concatenate([c64, s64], axis=1),
                           jnp.concatenate([-s64, c64], axis=1)], axis=0) * FFT_N1 ** -0.5).astype(BF16)
    twc, tws = _dft_cos_sin(SEQ, rows=FFT_N2, cols=FFT_N1)
    twc = jnp.repeat(twc, LANES, axis=1)
    tws = jnp.repeat(tws, LANES, axis=1)
    cctx, sctx = _dft_cos_sin(CTX_LEN)
    fctx = (jnp.concatenate([cctx, -sctx], axis=0) * CTX_LEN ** -0.5).astype(BF16)
    cs = (jnp.concatenate([c128, s128], axis=0) * LANES ** -0.5).astype(BF16)
    avg = jnp.kron(jnp.eye(8, dtype=F32), jnp.full((HEAD_DIM, HEAD_DIM), 1.0 / HEAD_DIM, F32)).astype(BF16)

    ct = jnp.zeros((D_MODEL, 8), F32).at[:, 0:BATCH].set(c.T).at[:, BATCH].set(c_ctx)
    mods_all = _ada_call(ct, w_ada, b_ada).reshape(DEPTH, 8, N_MOD, D_MODEL)[:, :3]

    h = x.reshape(N_LAT, D_MODEL)
    q_mul = HEAD_DIM ** -0.5 * math.log2(math.e)
    q_scale = jnp.array([q_mul, 1.0, q_mul, 1.0], F32)[:, None]
    for l in range(DEPTH):
        last = l == DEPTH - 1
        mods = mods_all[l]
        ng = norm_g[l].reshape(3, 1, D_MODEL)
        wi = [ffn_wi[l, k].astype(BF16) for k in range(2)]
        wo = [ffn_wo[l, k].astype(BF16) for k in range(2)]
        w_proj = w_in[l][:, :N_PROJ].astype(BF16)
        w_gate = w_in[l][:, N_PROJ:].astype(BF16)
        gt = jnp.tile(qk_g[l], (1, 2)) * q_scale
        lam_init = 0.8 - 0.6 * math.exp(-0.3 * l)
        subln = diff_subln_g[l][None, :]

        h = _ffn_call(h, mods, ng[0], wi[0], wo[0], k=0, n_tiles=NT_ALL,
                      h_ctx=ctx.reshape(N_CTX, D_MODEL) if l == 0 else None)
        qa, ka, va, qb, kb, vb, uc = _proj_call(h, mods, ng[1], w_proj, gt, rc, rs, avg)
        ya = _attn_a_call(qa, ka, va, ctx_queries=False)
        yb = _attn_b_call(qb, kb, vb, diff_lam[l], subln, ctx_queries=False, lam_init=lam_init)
        yr, yi = _fft1_call(uc.reshape(R_ALL * 512 // (FFT_N1 * 512), FFT_N1 * 512), f1, twc, tws)
        zr, zi = _fft2_call(yr, yi, g2)
        lat = (ya, yb, zr.reshape(N_LAT, 512), zi.reshape(N_LAT, 512))
        ctx_parts = None
        if not last:
            ctx_parts = (_attn_a_call(qa, ka, va, ctx_queries=True),
                         _attn_b_call(qb, kb, vb, diff_lam[l], subln, ctx_queries=True, lam_init=lam_init),
                         *_fft_ctx_call(uc, fctx))
        h = _merge_call(h, mods, ng[1], lat, ctx_parts, w_gate, w_branch[l].astype(BF16), w_out[l].astype(BF16), cs)
        n_tiles = NT_LAT if last else NT_ALL
        h = _ffn_call(h, mods, ng[2], wi[1], wo[1], k=2, n_tiles=n_tiles)
    return h.reshape(BATCH, SEQ, D_MODEL)
```

```python
import functools
import math

import jax
import jax.numpy as jnp
from jax import lax
from jax.experimental import pallas as pl
from jax.experimental.pallas import tpu as pltpu

F32 = jnp.float32
BF16 = jnp.bfloat16

D_MODEL = 1024
BATCH = 2
SEQ = 8192
DEPTH = 2
GRID_W = 64
CTX_LEN = 256
HEAD_DIM = 64
D_FF = 2816
N_MOD = 9
ROPE_THETA = 10000.0
EPS = 1e-6

LANES = 128
N_LAT = BATCH * SEQ
N_CTX = BATCH * CTX_LEN
R_ALL = N_LAT + N_CTX
TM = 512
NT_LAT = N_LAT // TM
NT_ALL = R_ALL // TM
TILES_PER_BATCH = SEQ // TM
FF_CHUNK = 256
TQ_A = 512
TQ_B = 512
CTX_BLK0 = N_LAT // CTX_LEN
TK = 1024
ADA_TN = 1152
FFT_N1 = 64
FFT_N2 = 128
FFT_TN = 4096
FFT_KB = 16
VMEM_LIMIT = 56 * 1024 * 1024


def _cparams(n_axes):
    return pltpu.CompilerParams(dimension_semantics=("arbitrary",) * n_axes,
                                vmem_limit_bytes=VMEM_LIMIT)


def _resident(shape):
    nd = len(shape)
    return pl.BlockSpec(shape, lambda *_: (0,) * nd, pipeline_mode=pl.Buffered(1))


def _mod_set(i):
    return jnp.minimum(i // TILES_PER_BATCH, 2)


def _rms_mod(h, g, shift, scale):
    y = h * lax.rsqrt(jnp.mean(h * h, axis=-1, keepdims=True) + EPS) * g
    return y * (1.0 + scale) + shift


def _ada_kernel(ct_ref, w_ref, b_ref, o_ref):
    ct = ct_ref[...]
    s = ct * jax.nn.sigmoid(ct)
    w = w_ref[0]
    rows = [jnp.sum(w * s[:, r:r + 1], axis=0, keepdims=True) for r in range(3)]
    rows.append(jnp.zeros((5, ADA_TN), F32))
    o_ref[0] = jnp.concatenate(rows, axis=0) + b_ref[0]


def _ada_call(ct, w_ada, b_ada):
    n_col = N_MOD * D_MODEL
    return pl.pallas_call(
        _ada_kernel,
        grid=(DEPTH, n_col // ADA_TN),
        in_specs=[pl.BlockSpec((D_MODEL, 8), lambda l, j: (0, 0)),
                  pl.BlockSpec((1, D_MODEL, ADA_TN), lambda l, j: (l, 0, j)),
                  pl.BlockSpec((1, 1, ADA_TN), lambda l, j: (l, 0, j))],
        out_specs=pl.BlockSpec((1, 8, ADA_TN), lambda l, j: (l, 0, j)),
        out_shape=jax.ShapeDtypeStruct((DEPTH, 8, n_col), F32),
        compiler_params=_cparams(2),
        name="ada",
    )(ct, w_ada, b_ada.reshape(DEPTH, 1, n_col))


def _ffn_kernel(*refs, k, split_rows):
    if split_rows:
        h_ref, hc_ref, mods_ref, g_ref, wi_ref, wo_ref, o_ref = refs
        h = jnp.where(pl.program_id(0) == NT_LAT, hc_ref[...], h_ref[...])
    else:
        h_ref, mods_ref, g_ref, wi_ref, wo_ref, o_ref = refs
        h = h_ref[...]
    m = mods_ref[0]
    hn = _rms_mod(h, g_ref[...], m[3 * k:3 * k + 1], m[3 * k + 1:3 * k + 2]).astype(BF16)
    acc = None
    for lo in range(0, D_FF, FF_CHUNK):
        cols = slice(lo, min(lo + FF_CHUNK, D_FF))
        g = jnp.dot(hn, wi_ref[:, cols], preferred_element_type=F32)
        u = jnp.dot(hn, wi_ref[:, D_FF + cols.start:D_FF + cols.stop], preferred_element_type=F32)
        a = (g * jax.nn.sigmoid(g) * u).astype(BF16)
        o = jnp.dot(a, wo_ref[cols, :], preferred_element_type=F32)
        acc = o if acc is None else acc + o
    o_ref[...] = h + 0.5 * m[3 * k + 2:3 * k + 3] * acc


def _ffn_call(h, mods, g, wi, wo, *, k, n_tiles, h_ctx=None):
    rows = n_tiles * TM
    split_rows = h_ctx is not None
    if split_rows:
        row_specs = [pl.BlockSpec((TM, D_MODEL), lambda i: (jnp.minimum(i, NT_LAT - 1), 0)),
                     _resident((N_CTX, D_MODEL))]
    else:
        row_specs = [pl.BlockSpec((TM, D_MODEL), lambda i: (i, 0))]
    return pl.pallas_call(
        functools.partial(_ffn_kernel, k=k, split_rows=split_rows),
        grid=(n_tiles,),
        in_specs=row_specs
                 + [pl.BlockSpec((1, N_MOD, D_MODEL), lambda i: (_mod_set(i), 0, 0)),
                    _resident((1, D_MODEL)),
                    _resident((D_MODEL, 2 * D_FF)),
                    _resident((D_FF, D_MODEL))],
        out_specs=pl.BlockSpec((TM, D_MODEL), lambda i: (i, 0)),
        out_shape=jax.ShapeDtypeStruct((rows, D_MODEL), F32),
        compiler_params=_cparams(1),
        name=f"ffn{k}",
    )(h, *((h_ctx,) if split_rows else ()), mods, g, wi, wo)


N_PROJ = 2816


def _proj_kernel(h_ref, mods_ref, g_ref, w_ref, gt_ref, rc_ref, rs_ref, avg_ref,
                 qa_ref, ka_ref, va_ref, qb_ref, kb_ref, vb_ref, uc_ref):
    h = h_ref[...]
    m = mods_ref[0]
    hn = _rms_mod(h, g_ref[...], m[3:4], m[4:5]).astype(BF16)
    rc = rc_ref[...]
    rs = rs_ref[...]
    gt = gt_ref[...]
    lane = lax.broadcasted_iota(jnp.int32, (TM, LANES), 1)
    lo = lane < HEAD_DIM
    first_half = (lane % HEAD_DIM) < HEAD_DIM // 2

    def rope(y):
        swapped = jnp.where(first_half, pltpu.roll(y, 96, 1), pltpu.roll(y, 32, 1))
        return y * rc + swapped * rs

    def project(lo_col, hi_col):
        return jnp.dot(hn, w_ref[:, lo_col:hi_col], preferred_element_type=F32)

    def slabs(z):
        return [z[:, LANES * j:LANES * (j + 1)] for j in range(z.shape[1] // LANES)]

    def qk_norm(z, g):
        wd = z.shape[1]
        ms = jnp.dot((z * z).astype(BF16), avg_ref[:wd, :wd], preferred_element_type=F32)
        y = z * lax.rsqrt(ms + EPS)
        return [rope(s * g) for s in slabs(y)]

    def one_per_slab(pair, fill):
        return [jnp.where(lo, pair, fill), jnp.where(lo, pltpu.roll(pair, HEAD_DIM, 1), fill)]

    def store(ref, parts):
        ref[...] = jnp.concatenate(parts, axis=1).astype(BF16)

    store(qa_ref, [s for pair in qk_norm(project(0, 512), gt[0:1]) for s in one_per_slab(pair, 0.0)])
    z = project(512, 768)
    store(ka_ref, one_per_slab(qk_norm(z[:, :LANES], gt[1:2])[0], 0.0))
    store(va_ref, one_per_slab(z[:, LANES:], 1.0))
    store(qb_ref, qk_norm(project(768, 1280), gt[2:3]))
    store(kb_ref, qk_norm(project(1280, 1792), gt[3:4]))
    ones = jnp.ones((TM, LANES), F32)
    store(vb_ref, [part for v in slabs(project(1792, 2304)) for part in (v, ones)])
    store(uc_ref, [project(2304, N_PROJ)])


def _proj_call(h, mods, g, w, gt, rc, rs, avg):
    widths = (1024, 256, 256, 512, 512, 1024, 512)
    return pl.pallas_call(
        _proj_kernel,
        grid=(NT_ALL,),
        in_specs=[pl.BlockSpec((TM, D_MODEL), lambda i: (i, 0)),
                  pl.BlockSpec((1, N_MOD, D_MODEL), lambda i: (_mod_set(i), 0, 0)),
                  _resident((1, D_MODEL)),
                  _resident((D_MODEL, N_PROJ)),
                  _resident((4, LANES)),
                  pl.BlockSpec((TM, LANES), lambda i: (jnp.where(i < NT_LAT, i % TILES_PER_BATCH, TILES_PER_BATCH), 0)),
                  pl.BlockSpec((TM, LANES), lambda i: (jnp.where(i < NT_LAT, i % TILES_PER_BATCH, TILES_PER_BATCH), 0)),
                  _resident((512, 512))],
        out_specs=[pl.BlockSpec((TM, wd), lambda i: (i, 0)) for wd in widths],
        out_shape=[jax.ShapeDtypeStruct((R_ALL, wd), BF16) for wd in widths],
        compiler_params=_cparams(1),
        name="proj",
    )(h, mods, g, w, gt, rc, rs, avg)


def _attn_pipeline(chunks, qs, k_of, v_of, s_ref, p_ref, al_ref, m_ref, acc_ref):
    width = lambda c: c[2]

    def qk(i):
        for st, q in enumerate(qs):
            s_ref[st, i % 2, :, :width(chunks[i])] = lax.dot_general(
                q, k_of(chunks[i], st), (((1,), (1,)), ((), ())), preferred_element_type=F32)

    def softmax(i):
        for st in range(len(qs)):
            s = s_ref[st, i % 2, :, :width(chunks[i])]
            m_old = m_ref[st]
            m_new = jnp.maximum(m_old, jnp.max(s, axis=-1, keepdims=True))
            al_ref[st, i % 2] = jnp.exp2(m_old - m_new)
            for j in range(width(chunks[i]) // LANES):
                cols = slice(LANES * j, LANES * (j + 1))
                p_ref[st, i % 2, :, cols] = jnp.exp2(s[:, cols] - m_new).astype(BF16)
            m_ref[st] = m_new

    def pv(i):
        for st in range(len(qs)):
            o = jnp.dot(p_ref[st, i % 2, :, :width(chunks[i])], v_of(chunks[i], st), preferred_element_type=F32)
            al = al_ref[st, i % 2]
            for j in range(acc_ref.shape[2] // LANES):
                cols = slice(LANES * j, LANES * (j + 1))
                acc_ref[st, :, cols] = al * acc_ref[st, :, cols] + o[:, cols]

    m_ref[...] = jnp.full_like(m_ref, -1e30)
    acc_ref[...] = jnp.zeros_like(acc_ref)
    n = len(chunks)
    qk(0)
    if n > 1:
        qk(1)
    softmax(0)
    for i in range(n - 2):
        pv(i)
        qk(i + 2)
        softmax(i + 1)
    if n > 1:
        pv(n - 2)
        softmax(n - 1)
    pv(n - 1)


def _chunk_list(ctx_queries):
    ctx = (True, 0, CTX_LEN)
    return [ctx] if ctx_queries else [(False, r, TK) for r in range(0, SEQ, TK)] + [ctx]


def _chunk_rows(c):
    return pl.ds(c[1], c[2])


def _split_attn_refs(refs, ctx_queries, n_extra):
    q_ref, kc_ref, vc_ref = refs[:3]
    kl_ref, vl_ref = (None, None) if ctx_queries else refs[3:5]
    rest = refs[3 if ctx_queries else 5:]
    return q_ref, kc_ref, vc_ref, kl_ref, vl_ref, rest[:n_extra], rest[n_extra], rest[n_extra + 1:]


def _attn_a_kernel(*refs, ctx_queries):
    q_ref, kc_ref, vc_ref, kl_ref, vl_ref, _, o_ref, scratch = _split_attn_refs(refs, ctx_queries, 0)
    tq = q_ref.shape[0]
    q4 = jnp.concatenate([q_ref[:, LANES * j:LANES * (j + 1)] for j in range(4)], axis=0)
    k_of = lambda c, st: (kc_ref if c[0] else kl_ref)[_chunk_rows(c), :]
    v_of = lambda c, st: (vc_ref if c[0] else vl_ref)[_chunk_rows(c), :]
    _attn_pipeline(_chunk_list(ctx_queries), [q4], k_of, v_of, *scratch)
    acc = scratch[-1][0]
    o = acc / pltpu.roll(acc, HEAD_DIM, 1)
    lo = lax.broadcasted_iota(jnp.int32, (tq, LANES), 1) < HEAD_DIM
    pairs = [jnp.where(lo, o[2 * j * tq:(2 * j + 1) * tq], pltpu.roll(o[(2 * j + 1) * tq:(2 * j + 2) * tq], HEAD_DIM, 1))
             for j in range(2)]
    o_ref[...] = jnp.concatenate(pairs, axis=1).astype(BF16)


def _attn_scratch(n_streams, rows, n_v):
    return [pltpu.VMEM((n_streams, 2, rows, TK), F32),
            pltpu.VMEM((n_streams, 2, rows, TK), BF16),
            pltpu.VMEM((n_streams, 2, rows, LANES), F32),
            pltpu.VMEM((n_streams, rows, LANES), F32),
            pltpu.VMEM((n_streams, rows, n_v), F32)]


def _attn_specs(ctx_queries, tq, n_groups, q_w, k_w, v_w, o_w):
    tq = CTX_LEN if ctx_queries else tq
    nq = SEQ // tq
    ctx_blk = lambda b, g, qi: (CTX_BLK0 + b, g)
    q_blk = ctx_blk if ctx_queries else (lambda b, g, qi: (b * nq + qi, g))
    o_blk = (lambda b, g, qi: (b, g)) if ctx_queries else q_blk
    specs = [pl.BlockSpec((tq, q_w), q_blk),
             pl.BlockSpec((CTX_LEN, k_w), ctx_blk),
             pl.BlockSpec((CTX_LEN, v_w), ctx_blk)]
    if not ctx_queries:
        lat_blk = lambda b, g, qi: (b, g)
        specs += [pl.BlockSpec((SEQ, k_w), lat_blk, pipeline_mode=pl.Buffered(1)),
                  pl.BlockSpec((SEQ, v_w), lat_blk, pipeline_mode=pl.Buffered(1))]
    grid = (BATCH, n_groups, 1 if ctx_queries else nq)
    return grid, specs, pl.BlockSpec((tq, o_w), o_blk), tq


def _attn_a_call(qa, ka, va, *, ctx_queries):
    grid, in_specs, out_spec, tq = _attn_specs(ctx_queries, TQ_A, 2, 512, LANES, LANES, 256)
    kv = (ka, va) if ctx_queries else (ka, va, ka, va)
    return pl.pallas_call(
        functools.partial(_attn_a_kernel, ctx_queries=ctx_queries),
        grid=grid,
        in_specs=in_specs,
        out_specs=out_spec,
        out_shape=jax.ShapeDtypeStruct((N_CTX if ctx_queries else N_LAT, 512), BF16),
        scratch_shapes=_attn_scratch(1, 4 * tq, LANES),
        compiler_params=_cparams(3),
        name="attn_a_ctx" if ctx_queries else "attn_a",
    )(qa, *kv)


def _attn_b_kernel(*refs, ctx_queries, lam_init):
    q_ref, kc_ref, vc_ref, kl_ref, vl_ref, (dl_ref, sg_ref), o_ref, scratch = _split_attn_refs(refs, ctx_queries, 2)
    tq = q_ref.shape[0]
    lo = lax.broadcasted_iota(jnp.int32, (tq, LANES), 1) < HEAD_DIM
    zero = jnp.zeros((tq, LANES), BF16)
    n_h = q_ref.shape[1] // LANES
    q2 = []
    for hh in range(n_h):
        qh = q_ref[:, LANES * hh:LANES * (hh + 1)]
        q2.append(jnp.concatenate([jnp.where(lo, qh, zero), jnp.where(lo, zero, qh)], axis=0))
    k_of = lambda c, hh: (kc_ref if c[0] else kl_ref)[_chunk_rows(c), LANES * hh:LANES * (hh + 1)]
    v_of = lambda c, hh: (vc_ref if c[0] else vl_ref)[_chunk_rows(c), 256 * hh:256 * (hh + 1)]
    _attn_pipeline(_chunk_list(ctx_queries), q2, k_of, v_of, *scratch)

    lp = dl_ref[...]
    lam = (jnp.exp(jnp.sum(lp[0:1] * lp[1:2], axis=-1, keepdims=True))
           - jnp.exp(jnp.sum(lp[2:3] * lp[3:4], axis=-1, keepdims=True)) + lam_init)
    outs = []
    for hh in range(n_h):
        acc = scratch[-1][hh]
        o = acc[:, :LANES] / acc[:, LANES:]
        d = o[:tq] - lam * o[tq:]
        y = d * lax.rsqrt(jnp.mean(d * d, axis=-1, keepdims=True) + EPS) * sg_ref[...]
        outs.append(y * (1.0 - lam_init))
    o_ref[...] = jnp.concatenate(outs, axis=1).astype(BF16)


def _attn_b_call(qb, kb, vb, diff_lam_l, subln_g, *, ctx_queries, lam_init):
    grid, in_specs, out_spec, tq = _attn_specs(ctx_queries, TQ_B, 4, LANES, LANES, 256, LANES)
    in_specs += [pl.BlockSpec((4, HEAD_DIM), lambda b, hp, qi: (0, 0)),
                 pl.BlockSpec((1, LANES), lambda b, hp, qi: (0, 0))]
    kv = (kb, vb) if ctx_queries else (kb, vb, kb, vb)
    return pl.pallas_call(
        functools.partial(_attn_b_kernel, ctx_queries=ctx_queries, lam_init=lam_init),
        grid=grid,
        in_specs=in_specs,
        out_specs=out_spec,
        out_shape=jax.ShapeDtypeStruct((N_CTX if ctx_queries else N_LAT, 512), BF16),
        scratch_shapes=_attn_scratch(1, 2 * tq, 256),
        compiler_params=_cparams(3),
        name="attn_b_ctx" if ctx_queries else "attn_b",
    )(qb, *kv, diff_lam_l, subln_g)


def _fft1_kernel(x_ref, f_ref, tc_ref, ts_ref, yr_ref, yi_ref):
    y = jnp.dot(f_ref[...], x_ref[...], preferred_element_type=F32)
    for t in range(FFT_TN // 512):
        c = jnp.concatenate([tc_ref[:, LANES * t:LANES * (t + 1)]] * 4, axis=1)
        s = jnp.concatenate([ts_ref[:, LANES * t:LANES * (t + 1)]] * 4, axis=1)
        a = y[:FFT_N2, 512 * t:512 * (t + 1)]
        b = y[FFT_N2:, 512 * t:512 * (t + 1)]
        yr_ref[:, 512 * t:512 * (t + 1)] = (a * c + b * s).astype(BF16)
        yi_ref[:, 512 * t:512 * (t + 1)] = (b * c - a * s).astype(BF16)


def _fft1_call(uc_view, f_stack, tc, ts):
    n_col = FFT_N1 * 512
    return pl.pallas_call(
        _fft1_kernel,
        grid=(BATCH, n_col // FFT_TN),
        in_specs=[pl.BlockSpec((FFT_N2, FFT_TN), lambda b, j: (b, j)),
                  pl.BlockSpec((2 * FFT_N2, FFT_N2), lambda b, j: (0, 0)),
                  pl.BlockSpec((FFT_N2, FFT_TN // 4), lambda b, j: (0, j)),
                  pl.BlockSpec((FFT_N2, FFT_TN // 4), lambda b, j: (0, j))],
        out_specs=[pl.BlockSpec((FFT_N2, FFT_TN), lambda b, j: (b, j))] * 2,
        out_shape=[jax.ShapeDtypeStruct((BATCH * FFT_N2, n_col), BF16)] * 2,
        compiler_params=_cparams(2),
        name="fft1",
    )(uc_view, f_stack, tc, ts)


def _fft2_kernel(yr_ref, yi_ref, g_ref, zr_ref, zi_ref):
    for t in range(FFT_KB):
        ys = jnp.concatenate([yr_ref[t], yi_ref[t]], axis=0)
        z = jnp.dot(g_ref[...], ys, preferred_element_type=F32)
        zr_ref[:, 512 * t:512 * (t + 1)] = z[:FFT_N1].astype(BF16)
        zi_ref[:, 512 * t:512 * (t + 1)] = z[FFT_N1:].astype(BF16)


def _fft2_call(yr, yi, g_stack):
    steps = FFT_N2 // FFT_KB
    return pl.pallas_call(
        _fft2_kernel,
        grid=(BATCH, steps),
        in_specs=[pl.BlockSpec((FFT_KB, FFT_N1, 512), lambda b, j: (b * steps + j, 0, 0)),
                  pl.BlockSpec((FFT_KB, FFT_N1, 512), lambda b, j: (b * steps + j, 0, 0)),
                  pl.BlockSpec((2 * FFT_N1, 2 * FFT_N1), lambda b, j: (0, 0))],
        out_specs=[pl.BlockSpec((FFT_N1, FFT_KB * 512), lambda b, j: (b, j))] * 2,
        out_shape=[jax.ShapeDtypeStruct((BATCH * FFT_N1, FFT_N2 * 512), BF16)] * 2,
        compiler_params=_cparams(2),
        name="fft2",
    )(yr.reshape(BATCH * FFT_N2, FFT_N1, 512), yi.reshape(BATCH * FFT_N2, FFT_N1, 512), g_stack)


def _fft_ctx_kernel(x_ref, f_ref, zr_ref, zi_ref):
    z = jnp.dot(f_ref[...], x_ref[...], preferred_element_type=F32)
    zr_ref[...] = z[:CTX_LEN].astype(BF16)
    zi_ref[...] = z[CTX_LEN:].astype(BF16)


def _fft_ctx_call(uc, f_stack):
    return pl.pallas_call(
        _fft_ctx_kernel,
        grid=(BATCH,),
        in_specs=[pl.BlockSpec((CTX_LEN, 512), lambda b: (CTX_BLK0 + b, 0)),
                  pl.BlockSpec((2 * CTX_LEN, CTX_LEN), lambda b: (0, 0))],
        out_specs=[pl.BlockSpec((CTX_LEN, 512), lambda b: (b, 0))] * 2,
        out_shape=[jax.ShapeDtypeStruct((N_CTX, 512), BF16)] * 2,
        compiler_params=_cparams(1),
        name="fft_ctx",
    )(uc, f_stack)


def _merge_kernel(h_ref, mods_ref, g_ref, *rest, with_ctx):
    lat = [r[...] for r in rest[:4]]
    if with_ctx:
        is_ctx = pl.program_id(0) == NT_LAT
        ya, yb, zr, zi = [jnp.where(is_ctx, c[...], x) for c, x in zip(rest[4:8], lat)]
        wg_ref, wb_ref, wo_ref, cs_ref, o_ref = rest[8:]
    else:
        ya, yb, zr, zi = lat
        wg_ref, wb_ref, wo_ref, cs_ref, o_ref = rest[4:]
    h = h_ref[...]
    m = mods_ref[0]
    hn = _rms_mod(h, g_ref[...], m[3:4], m[4:5]).astype(BF16)
    yc = jnp.concatenate(
        [jnp.dot(jnp.concatenate([zr[:, LANES * g:LANES * (g + 1)], zi[:, LANES * g:LANES * (g + 1)]], axis=1),
                 cs_ref[...], preferred_element_type=F32) for g in range(4)], axis=1).astype(BF16)
    branches = (ya, yb, yc)
    mix = None
    for i in range(3):
        gl = jnp.dot(hn, wg_ref[:, D_MODEL * i:D_MODEL * (i + 1)], preferred_element_type=F32)
        pr = jnp.dot(branches[i], wb_ref[i], preferred_element_type=F32)
        t = jax.nn.sigmoid(gl) * pr
        mix = t if mix is None else mix + t
    o_ref[...] = h + m[5:6] * jnp.dot(mix.astype(BF16), wo_ref[...], preferred_element_type=F32)


def _merge_call(h, mods, g, lat, ctx, wg, wb, wo, cs):
    with_ctx = ctx is not None
    n_tiles = NT_ALL if with_ctx else NT_LAT
    row = lambda i: (i, 0)
    lat_row = lambda i: (jnp.minimum(i, NT_LAT - 1), 0)
    ctx_specs = [_resident((N_CTX, 512))] * 4 if with_ctx else []
    return pl.pallas_call(
        functools.partial(_merge_kernel, with_ctx=with_ctx),
        grid=(n_tiles,),
        in_specs=[pl.BlockSpec((TM, D_MODEL), row),
                  pl.BlockSpec((1, N_MOD, D_MODEL), lambda i: (_mod_set(i), 0, 0)),
                  _resident((1, D_MODEL))]
                 + [pl.BlockSpec((TM, 512), lat_row)] * 4
                 + ctx_specs
                 + [_resident((D_MODEL, 3 * D_MODEL)),
                    _resident((3, 512, D_MODEL)),
                    _resident((D_MODEL, D_MODEL)),
                    _resident((2 * LANES, LANES))],
        out_specs=pl.BlockSpec((TM, D_MODEL), row),
        out_shape=jax.ShapeDtypeStruct((n_tiles * TM, D_MODEL), F32),
        compiler_params=_cparams(1),
        name="merge",
    )(h, mods, g, *lat, *(ctx or ()), wg, wb, wo, cs)


def _dft_cos_sin(n, rows=None, cols=None):
    j = jnp.arange(n if rows is None else rows, dtype=jnp.int32)[:, None]
    k = jnp.arange(n if cols is None else cols, dtype=jnp.int32)[None, :]
    ang = ((j * k) % n).astype(F32) * (2.0 * math.pi / n)
    return jnp.cos(ang), jnp.sin(ang)


def _rope_tables():
    rows = SEQ // GRID_W
    row = jnp.repeat(jnp.arange(rows, dtype=F32), GRID_W)
    col = jnp.tile(jnp.arange(GRID_W, dtype=F32), rows)
    n_freq = HEAD_DIM // 4
    inv = jnp.power(ROPE_THETA, -jnp.arange(n_freq, dtype=F32) / n_freq)
    ang = jnp.concatenate([row[:, None] * inv, col[:, None] * inv], axis=-1)
    cos, sin = jnp.cos(ang), jnp.sin(ang)
    rc = jnp.tile(cos, (1, 4))
    rs = jnp.tile(jnp.concatenate([-sin, sin], axis=-1), (1, 2))
    rc = jnp.concatenate([rc, jnp.ones((TM, LANES), F32)], axis=0)
    rs = jnp.concatenate([rs, jnp.zeros((TM, LANES), F32)], axis=0)
    return rc, rs


def kernel(x, c, ctx, c_ctx, w_ada, b_ada, norm_g, ffn_wi, ffn_wo, w_in, qk_g, diff_lam, diff_subln_g, w_branch, w_out):
    rc, rs = _rope_tables()
    c128, s128 = _dft_cos_sin(FFT_N2)
    f1 = (jnp.concatenate([c128, -s128], axis=0) * FFT_N2 ** -0.5).astype(BF16)
    c64, s64 = _dft_cos_sin(FFT_N1)
    g2 = (jnp.concatenate([jnp.concatenate([c64, s64], axis=1),
                           jnp.concatenate([-s64, c64], axis=1)], axis=0) * FFT_N1 ** -0.5).astype(BF16)
    twc, tws = _dft_cos_sin(SEQ, rows=FFT_N2, cols=FFT_N1)
    twc = jnp.repeat(twc, LANES, axis=1)
    tws = jnp.repeat(tws, LANES, axis=1)
    cctx, sctx = _dft_cos_sin(CTX_LEN)
    fctx = (jnp.concatenate([cctx, -sctx], axis=0) * CTX_LEN ** -0.5).astype(BF16)
    cs = (jnp.concatenate([c128, s128], axis=0) * LANES ** -0.5).astype(BF16)
    avg = jnp.kron(jnp.eye(8, dtype=F32), jnp.full((HEAD_DIM, HEAD_DIM), 1.0 / HEAD_DIM, F32)).astype(BF16)

    ct = jnp.zeros((D_MODEL, 8), F32).at[:, 0:BATCH].set(c.T).at[:, BATCH].set(c_ctx)
    mods_all = _ada_call(ct, w_ada, b_ada).reshape(DEPTH, 8, N_MOD, D_MODEL)[:, :3]

    h = x.reshape(N_LAT, D_MODEL)
    q_mul = HEAD_DIM ** -0.5 * math.log2(math.e)
    q_scale = jnp.array([q_mul, 1.0, q_mul, 1.0], F32)[:, None]
    for l in range(DEPTH):
        last = l == DEPTH - 1
        mods = mods_all[l]
        ng = norm_g[l].reshape(3, 1, D_MODEL)
        wi = [ffn_wi[l, k].astype(BF16) for k in range(2)]
        wo = [ffn_wo[l, k].astype(BF16) for k in range(2)]
        w_proj = w_in[l][:, :N_PROJ].astype(BF16)
        w_gate = w_in[l][:, N_PROJ:].astype(BF16)
        gt = jnp.tile(qk_g[l], (1, 2)) * q_scale
        lam_init = 0.8 - 0.6 * math.exp(-0.3 * l)
        subln = diff_subln_g[l][None, :]

        h = _ffn_call(h, mods, ng[0], wi[0], wo[0], k=0, n_tiles=NT_ALL,
                      h_ctx=ctx.reshape(N_CTX, D_MODEL) if l == 0 else None)
        qa, ka, va, qb, kb, vb, uc = _proj_call(h, mods, ng[1], w_proj, gt, rc, rs, avg)
        ya = _attn_a_call(qa, ka, va, ctx_queries=False)
        yb = _attn_b_call(qb, kb, vb, diff_lam[l], subln, ctx_queries=False, lam_init=lam_init)
        yr, yi = _fft1_call(uc.reshape(R_ALL * 512 // (FFT_N1 * 512), FFT_N1 * 512), f1, twc, tws)
        zr, zi = _fft2_call(yr, yi, g2)
        lat = (ya, yb, zr.reshape(N_LAT, 512), zi.reshape(N_LAT, 512))
        ctx_parts = None
        if not last:
            ctx_parts = (_attn_a_call(qa, ka, va, ctx_queries=True),
                         _attn_b_call(qb, kb, vb, diff_lam[l], subln, ctx_queries=True, lam_init=lam_init),
                         *_fft_ctx_call(uc, fctx))
        h = _merge_call(h, mods, ng[1], lat, ctx_parts, w_gate, w_branch[l].astype(BF16), w_out[l].astype(BF16), cs)
        n_tiles = NT_LAT if last else NT_ALL
        h = _ffn_call(h, mods, ng[2], wi[1], wo[1], k=2, n_tiles=n_tiles)
    return h.reshape(BATCH, SEQ, D_MODEL)
```
